```python
import math
import jax, jax.numpy as jnp
from jax import lax
import numpy as np

D_MODEL = 1024
BATCH = 2
SEQ = 16384
DEPTH = 2

N_MIXERS = 2
N_HEADS = 8
HEAD_DIM = D_MODEL // N_HEADS
Q_BLOCK = 128
K_SUB = 128
N_EXPERTS = 64
TOP_K = 8
EXPERT_DIM = D_MODEL // 4
SHARED_DIM = EXPERT_DIM
ROUTED_SCALE = 2.5
DISPATCH_BLOCK = 128
LN_EPS = 1e-5
DEEPNORM_ALPHA = (2 * DEPTH) ** 0.25
DEEPNORM_BETA = (8 * DEPTH) ** -0.25
N_FOX = (DEPTH + 1) // 2
N_SB = DEPTH // 2
ATTN_SCALE = 1.0 / math.sqrt(HEAD_DIM)

kernel_name = "hybrid_fox_stickbreaking_moe_deepnorm_adaln"


def layer_norm(x, g, b):
    xf = x.astype(jnp.float32)
    mu = jnp.mean(xf, axis=-1, keepdims=True)
    var = jnp.mean(jnp.square(xf - mu), axis=-1, keepdims=True)
    y = (xf - mu) * lax.rsqrt(var + LN_EPS) * g.astype(jnp.float32) + b.astype(jnp.float32)
    return y.astype(x.dtype)


def modulate(x, shift, scale):
    return x * (1.0 + scale[:, None, :]) + shift[:, None, :]


def split_heads(t):
    b, s, _ = t.shape
    return t.reshape(b, s, N_HEADS, HEAD_DIM).transpose(0, 2, 1, 3)


def merge_heads(o):
    b, h, s, d = o.shape
    return o.transpose(0, 2, 1, 3).reshape(b, s, h * d)


def causal_masks(i, L):
    t = i * Q_BLOCK + jnp.arange(Q_BLOCK)
    s = jnp.arange(L)
    return s[None, :] <= t[:, None], s[None, :] < t[:, None]


def fox_attention(h, w_in, b_f, w_o):
    B, S, D = h.shape
    proj = h @ w_in
    q, k, v, fl = jnp.split(proj, [D, 2 * D, 3 * D], axis=-1)
    q, k, v = split_heads(q), split_heads(k), split_heads(v)
    log_f = jax.nn.log_sigmoid((fl + b_f).astype(jnp.float32))
    F = jnp.cumsum(log_f, axis=1).transpose(0, 2, 1)
    outs = []
    for i in range(S // Q_BLOCK):
        L = (i + 1) * Q_BLOCK
        qi, Fi = q[:, :, i * Q_BLOCK:L], F[:, :, i * Q_BLOCK:L]
        causal, _ = causal_masks(i, L)
        logits = (jnp.einsum('bhqd,bhkd->bhqk', qi, k[:, :, :L], preferred_element_type=jnp.float32) * ATTN_SCALE
                  + Fi[..., None] - F[:, :, None, :L])
        p = jax.nn.softmax(jnp.where(causal, logits, -jnp.inf), axis=-1).astype(v.dtype)
        outs.append(jnp.einsum('bhqk,bhkd->bhqd', p, v[:, :, :L]))
    o = jnp.concatenate(outs, axis=2)
    return merge_heads(o) @ w_o


def exclusive_suffix_sum(u):
    shp = u.shape
    nk = shp[-1] // K_SUB
    ub = u.reshape(shp[:-1] + (nk, K_SUB))
    within = lax.cumsum(ub, axis=ub.ndim - 1, reverse=True)
    totals = within[..., 0]
    later = lax.cumsum(totals, axis=totals.ndim - 1, reverse=True) - totals
    return (within + later[..., None] - ub).reshape(shp)


def stick_breaking_attention(h, w_in, w_o):
    B, S, D = h.shape
    proj = h @ w_in
    q, k, v = jnp.split(proj, [D, 2 * D], axis=-1)
    q, k, v = split_heads(q), split_heads(k), split_heads(v)
    outs = []
    for i in range(S // Q_BLOCK):
        L = (i + 1) * Q_BLOCK
        qi = q[:, :, i * Q_BLOCK:L]
        _, strict = causal_masks(i, L)
        z = jnp.einsum('bhqd,bhkd->bhqk', qi, k[:, :, :L], preferred_element_type=jnp.float32) * ATTN_SCALE
        log1m = jnp.where(strict, jax.nn.log_sigmoid(-z), 0.0)
        suffix = exclusive_suffix_sum(log1m)
        a = jnp.exp(jnp.where(strict, jax.nn.log_sigmoid(z) + suffix, -jnp.inf)).astype(v.dtype)
        outs.append(jnp.einsum('bhqk,bhkd->bhqd', a, v[:, :, :L]))
    o = jnp.concatenate(outs, axis=2)
    return merge_heads(o) @ w_o


def swiglu(x, w_gate, w_up, w_down):
    return (jax.nn.silu(x @ w_gate) * (x @ w_up)) @ w_down


def moe(h, router_w, router_bias, w_gate, w_up, w_down, s_gate, s_up, s_down):
    B, S, D = h.shape
    T = B * S
    xt = h.reshape(T, D)
    scores = jax.nn.sigmoid(jnp.matmul(xt, router_w, preferred_element_type=jnp.float32))
    _, idx = lax.top_k(scores + router_bias.astype(jnp.float32), TOP_K)
    wts = jnp.take_along_axis(scores, idx, axis=-1)
    wts = wts / jnp.sum(wts, axis=-1, keepdims=True) * ROUTED_SCALE

    n_assign = T * TOP_K
    e_flat = idx.reshape(-1).astype(jnp.int32)
    tok_flat = jnp.repeat(jnp.arange(T, dtype=jnp.int32), TOP_K)
    w_flat = wts.reshape(-1).astype(xt.dtype)
    order = jnp.argsort(e_flat)
    e_s, tok_s, w_s = e_flat[order], tok_flat[order], w_flat[order]
    counts = jax.ops.segment_sum(jnp.ones((n_assign,), jnp.int32), e_flat, num_segments=N_EXPERTS)
    start = jnp.cumsum(counts) - counts
    padded = (counts + DISPATCH_BLOCK - 1) // DISPATCH_BLOCK * DISPATCH_BLOCK
    pend = jnp.cumsum(padded)
    pstart = pend - padded
    dest = pstart[e_s] + (jnp.arange(n_assign, dtype=jnp.int32) - start[e_s])
    n_blocks = -(-n_assign // DISPATCH_BLOCK) + N_EXPERTS
    rows = n_blocks * DISPATCH_BLOCK
    row_tok = jnp.zeros((rows,), jnp.int32).at[dest].set(tok_s)
    row_w = jnp.zeros((rows,), xt.dtype).at[dest].set(w_s)
    block_exp = jnp.minimum(
        jnp.searchsorted(pend, jnp.arange(n_blocks, dtype=jnp.int32) * DISPATCH_BLOCK, side='right'),
        N_EXPERTS - 1)
    x_rows = xt[row_tok].reshape(n_blocks, DISPATCH_BLOCK, D)

    def step(carry, inp):
        e, xb = inp
        return carry, swiglu(xb, w_gate[e], w_up[e], w_down[e])

    _, out = lax.scan(step, None, (block_exp, x_rows))
    y = jnp.zeros_like(xt).at[row_tok].add(out.reshape(rows, D) * row_w[:, None])
    y = y + swiglu(xt, s_gate, s_up, s_down)
    return y.reshape(B, S, D)


def setup_inputs(seed: int = 0) -> dict:
    key = jax.random.key(seed)
    ks = jax.random.split(key, 24)
    D, H, E, F, FS = D_MODEL, N_HEADS, N_EXPERTS, EXPERT_DIM, SHARED_DIM

    def nrm(k, shape, scale):
        return jax.random.normal(k, shape, jnp.float32) * scale

    fox_col_scale = jnp.concatenate([jnp.ones((2 * D,)), jnp.full((D,), DEEPNORM_BETA), jnp.ones((H,))])
    sb_col_scale = jnp.concatenate([jnp.ones((2 * D,)), jnp.full((D,), DEEPNORM_BETA)])
    return {
        "x": nrm(ks[0], (BATCH, SEQ, D), 1.0),
        "c": nrm(ks[1], (BATCH, D), 1.0),
        "fox_w_in": nrm(ks[2], (N_FOX, D, 3 * D + H), D ** -0.5) * fox_col_scale,
        "fox_b_f": 3.0 + nrm(ks[3], (N_FOX, H), 0.5),
        "fox_w_o": nrm(ks[4], (N_FOX, D, D), D ** -0.5 * DEEPNORM_BETA),
        "sb_w_in": nrm(ks[5], (N_SB, D, 3 * D), D ** -0.5) * sb_col_scale,
        "sb_w_o": nrm(ks[6], (N_SB, D, D), D ** -0.5 * DEEPNORM_BETA),
        "ada_w": nrm(ks[7], (DEPTH, D, 6 * D), 0.1 * D ** -0.5),
        "ada_b": nrm(ks[8], (DEPTH, 6 * D), 0.01),
        "ln_attn_g": 1.0 + nrm(ks[9], (DEPTH, D), 0.02),
        "ln_attn_b": nrm(ks[10], (DEPTH, D), 0.02),
        "router_w": nrm(ks[11], (DEPTH, D, E), D ** -0.5),
        "router_bias": nrm(ks[12], (DEPTH, E), 0.01),
        "exp_w_gate": nrm(ks[13], (DEPTH, E, D, F), D ** -0.5),
        "exp_w_up": nrm(ks[14], (DEPTH, E, D, F), D ** -0.5),
        "exp_w_down": nrm(ks[15], (DEPTH, E, F, D), F ** -0.5 * DEEPNORM_BETA),
        "shared_w_gate": nrm(ks[16], (DEPTH, D, FS), D ** -0.5),
        "shared_w_up": nrm(ks[17], (DEPTH, D, FS), D ** -0.5),
        "shared_w_down": nrm(ks[18], (DEPTH, FS, D), FS ** -0.5 * DEEPNORM_BETA),
        "ln_ffn_g": 1.0 + nrm(ks[19], (DEPTH, D), 0.02),
        "ln_ffn_b": nrm(ks[20], (DEPTH, D), 0.02),
    }


def reference(x, c, fox_w_in, fox_b_f, fox_w_o, sb_w_in, sb_w_o, ada_w, ada_b,
              ln_attn_g, ln_attn_b, router_w, router_bias, exp_w_gate, exp_w_up, exp_w_down,
              shared_w_gate, shared_w_up, shared_w_down, ln_ffn_g, ln_ffn_b):
    c_act = jax.nn.silu(c)
    for i in range(DEPTH):
        mod = c_act @ ada_w[i] + ada_b[i]
        sh_a, sc_a, g_a, sh_m, sc_m, g_m = jnp.split(mod, 6, axis=-1)

        h = modulate(x, sh_a, sc_a)
        j = i // N_MIXERS
        if i % N_MIXERS == 0:
            y = fox_attention(h, fox_w_in[j], fox_b_f[j], fox_w_o[j])
        else:
            y = stick_breaking_attention(h, sb_w_in[j], sb_w_o[j])
        x = layer_norm(DEEPNORM_ALPHA * x + (1.0 + g_a)[:, None, :] * y, ln_attn_g[i], ln_attn_b[i])

        h = modulate(x, sh_m, sc_m)
        y = moe(h, router_w[i], router_bias[i], exp_w_gate[i], exp_w_up[i], exp_w_down[i],
                shared_w_gate[i], shared_w_up[i], shared_w_down[i])
        x = layer_norm(DEEPNORM_ALPHA * x + (1.0 + g_m)[:, None, :] * y, ln_ffn_g[i], ln_ffn_b[i])
    return x
```

```python
import functools
import math

import jax
import jax.numpy as jnp
from jax import lax
from jax.experimental import pallas as pl
from jax.experimental.pallas import tpu as pltpu
from jax.experimental.pallas import tpu_sc as plsc

F32 = jnp.float32
BF16 = jnp.bfloat16
I32 = jnp.int32
U32 = jnp.uint32

N_HEADS = 8
HEAD_DIM = 128
N_EXPERTS = 64
TOP_K = 8
ROUTED_SCALE = 2.5
LN_EPS = 1e-5
DEPTH = 2
DEEPNORM_ALPHA = (2 * DEPTH) ** 0.25
ATTN_SCALE = 1.0 / math.sqrt(HEAD_DIM)

LANES = 128
UNDERFLOW_LOG = 105.0
VMEM_LIMIT = 56 * 1024 * 1024

ROW_TILE = 512
ATTN_TILE = 512
SB_TILE = 256
EXPERT_BLOCK = 512
SC_WINDOW = 32
SC_SUBCORES = 16
SC_WORKERS = 2 * SC_SUBCORES


def _params(*sem):
    return pltpu.CompilerParams(dimension_semantics=sem, vmem_limit_bytes=VMEM_LIMIT)


def _log_sigmoid(z):
    return jnp.minimum(z, 0.0) - jnp.log1p(jnp.exp(-jnp.abs(z)))


def _split3(x):
    hi = x.astype(BF16)
    r = x - hi.astype(F32)
    mid = r.astype(BF16)
    lo = (r - mid.astype(F32)).astype(BF16)
    return hi, mid, lo


def _pack_halves(x):
    n = x.shape[1] // 2
    bits = lax.bitcast_convert_type(x.astype(BF16).astype(F32), U32)
    return (bits[:, :n] >> 16) | (bits[:, n:] & jnp.uint32(0xFFFF0000))


def _unpack_halves(w):
    lo = lax.bitcast_convert_type(w << 16, F32).astype(BF16)
    hi = lax.bitcast_convert_type(w & jnp.uint32(0xFFFF0000), F32).astype(BF16)
    return lo, hi


def _layer_norm(z, g, b):
    mu = jnp.mean(z, axis=-1, keepdims=True)
    d = z - mu
    var = jnp.mean(d * d, axis=-1, keepdims=True)
    return d * lax.rsqrt(var + LN_EPS) * g + b


def _ada_kernel(c_ref, w_ref, b_ref, o_ref):
    c = c_ref[...]
    act = c * jax.nn.sigmoid(c)
    o_ref[...] = jnp.dot(act, w_ref[...], preferred_element_type=F32,
                         precision=lax.Precision.HIGHEST) + b_ref[...]


def _ada_mod(c, ada_w, ada_b):
    B, D = c.shape
    L, _, N = ada_w.shape
    cp = jnp.zeros((8, D), F32).at[:B].set(c)
    out = pl.pallas_call(
        _ada_kernel,
        grid=(L, N // D),
        in_specs=[pl.BlockSpec((8, D), lambda l, j: (0, 0)),
                  pl.BlockSpec((None, D, D), lambda l, j: (l, 0, j)),
                  pl.BlockSpec((None, 1, D), lambda l, j: (l, 0, j))],
        out_specs=pl.BlockSpec((None, 8, D), lambda l, j: (l, 0, j)),
        out_shape=jax.ShapeDtypeStruct((L, 8, N), F32),
        compiler_params=_params("parallel", "parallel"),
        name="ada_mod",
    )(cp, ada_w, ada_b.reshape(L, 1, N))
    return out[:, :B]


def _inproj_kernel(x_ref, sh_ref, sc_ref, w_ref, o_ref):
    h = (x_ref[...] * (1.0 + sc_ref[...]) + sh_ref[...]).astype(BF16)
    D = h.shape[1]
    for c in range(o_ref.shape[1] // D):
        o_ref[:, c * D:(c + 1) * D] = jnp.dot(
            h, w_ref[:, c * D:(c + 1) * D], preferred_element_type=F32).astype(BF16)


def _inproj_forget_kernel(x_ref, sh_ref, sc_ref, w_ref, wf_ref, bf_ref, o_ref, f_ref, carry_ref):
    tm = x_ref.shape[0]
    h = (x_ref[...] * (1.0 + sc_ref[...]) + sh_ref[...]).astype(BF16)
    D = h.shape[1]
    for c in range(o_ref.shape[1] // D):
        o_ref[:, c * D:(c + 1) * D] = jnp.dot(
            h, w_ref[:, c * D:(c + 1) * D], preferred_element_type=F32).astype(BF16)

    @pl.when(pl.program_id(1) == 0)
    def _():
        carry_ref[...] = jnp.zeros_like(carry_ref)

    logf = _log_sigmoid(jnp.dot(h, wf_ref[...], preferred_element_type=F32) + bf_ref[...])
    row = lax.broadcasted_iota(I32, (tm, tm), 0)
    col = lax.broadcasted_iota(I32, (tm, tm), 1)
    tri = jnp.where(row >= col, 1.0, 0.0).astype(BF16)
    hi, mid, lo = _split3(logf)
    cs = (jnp.dot(tri, hi, preferred_element_type=F32)
          + jnp.dot(tri, mid, preferred_element_type=F32)
          + jnp.dot(tri, lo, preferred_element_type=F32))
    cum = cs + carry_ref[...]
    f_ref[...] = cum
    carry_ref[...] = cum[tm - 1:tm, :]


def _in_projection(x, shift, scale, w_bf16, forget=None):
    B, S, D = x.shape
    N = w_bf16.shape[1]
    tm = min(ROW_TILE, S)
    row_spec = pl.BlockSpec((None, tm, D), lambda b, i: (b, i, 0))
    vec_spec = pl.BlockSpec((None, 1, D), lambda b, i: (b, 0, 0))
    w_spec = pl.BlockSpec((D, N), lambda b, i: (0, 0))
    o_spec = pl.BlockSpec((None, tm, N), lambda b, i: (b, i, 0))
    sh = shift.reshape(B, 1, D)
    sc = scale.reshape(B, 1, D)
    if forget is None:
        return pl.pallas_call(
            _inproj_kernel,
            grid=(B, S // tm),
            in_specs=[row_spec, vec_spec, vec_spec, w_spec],
            out_specs=o_spec,
            out_shape=jax.ShapeDtypeStruct((B, S, N), BF16),
            compiler_params=_params("parallel", "parallel"),
            name="in_projection",
        )(x, sh, sc, w_bf16)
    wf, bf = forget
    return pl.pallas_call(
        _inproj_forget_kernel,
        grid=(B, S // tm),
        in_specs=[row_spec, vec_spec, vec_spec, w_spec,
                  pl.BlockSpec((D, LANES), lambda b, i: (0, 0)),
                  pl.BlockSpec((1, LANES), lambda b, i: (0, 0))],
        out_specs=[o_spec, pl.BlockSpec((None, tm, LANES), lambda b, i: (b, i, 0))],
        out_shape=[jax.ShapeDtypeStruct((B, S, N), BF16),
                   jax.ShapeDtypeStruct((B, S, LANES), F32)],
        scratch_shapes=[pltpu.VMEM((1, LANES), F32)],
        compiler_params=_params("parallel", "arbitrary"),
        name="in_projection_forget",
    )(x, sh, sc, w_bf16, wf, bf)


def _fox_kernel(js_ref, q_ref, k_ref, v_ref, fk_ref, o_ref, m_ref, l_ref, acc_ref, *, n_heads):
    t = q_ref.shape[0]
    b, h, i = pl.program_id(0), pl.program_id(1), pl.program_id(2)
    nq = pl.num_programs(2)
    m_ref[...] = jnp.full(m_ref.shape, -jnp.inf, F32)
    l_ref[...] = jnp.zeros_like(l_ref)
    acc_ref[...] = jnp.zeros_like(acc_ref)
    q = q_ref[...]

    def block(j, masked):
        ks = pl.multiple_of(j * t, t)
        k = k_ref[pl.ds(ks, t), :]
        v = v_ref[pl.ds(ks, t), :]
        s = lax.dot_general(q, k, (((1,), (1,)), ((), ())), preferred_element_type=F32)
        s = s - fk_ref[:, pl.ds(ks, t)]
        if masked:
            row = lax.broadcasted_iota(I32, (t, t), 0)
            col = lax.broadcasted_iota(I32, (t, t), 1)
            s = jnp.where(col <= row, s, -jnp.inf)
        m_prev = m_ref[...]
        m_new = jnp.maximum(m_prev, jnp.max(s, axis=1, keepdims=True))
        alpha = jnp.exp(m_prev - m_new)
        p = jnp.exp(s - m_new)
        l_ref[...] = alpha * l_ref[...] + jnp.sum(p, axis=1, keepdims=True)
        acc_ref[...] = alpha * acc_ref[...] + jnp.dot(p.astype(BF16), v, preferred_element_type=F32)
        m_ref[...] = m_new

    def body(j, carry):
        block(j, False)
        return carry

    lax.fori_loop(js_ref[(b * n_heads + h) * nq + i], i, body, 0)
    block(i, True)
    o_ref[...] = (acc_ref[...] / l_ref[...]).astype(o_ref.dtype)


def _fox_first_blocks(proj, fcum, n_heads, t):
    B, S, _ = proj.shape
    D = n_heads * HEAD_DIM
    q = proj[..., :D].astype(F32).reshape(B, S, n_heads, HEAD_DIM)
    k = proj[..., D:2 * D].astype(F32).reshape(B, S, n_heads, HEAD_DIM)
    qn = jnp.sqrt(jnp.sum(q * q, axis=-1))
    kn = jnp.sqrt(jnp.max(jnp.sum(k * k, axis=-1), axis=1))
    diag = jnp.sum(q * k, axis=-1)
    f = fcum[..., :n_heads]
    need = qn * kn[:, None, :] * 1.001 - diag + f + (UNDERFLOW_LOG + 1.0)
    nq = S // t
    need = jnp.max(need.reshape(B, nq, t, n_heads), axis=2)
    f_end = f.reshape(B, nq, t, n_heads)[:, :, t - 1, :]
    skippable = f_end[:, None, :, :] > need[:, :, None, :]
    first = jnp.sum(skippable.astype(I32), axis=2)
    first = jnp.minimum(first, jnp.arange(nq, dtype=I32)[None, :, None])
    return first.transpose(0, 2, 1).reshape(-1)


def _fox_attention(proj, fcum, n_heads):
    B, S, _ = proj.shape
    D = n_heads * HEAD_DIM
    t = min(ATTN_TILE, S)
    nq = S // t
    first = _fox_first_blocks(proj, fcum, n_heads, t)
    fk = fcum[..., :n_heads].transpose(0, 2, 1).reshape(B, n_heads, 1, S)
    grid_spec = pltpu.PrefetchScalarGridSpec(
        num_scalar_prefetch=1,
        grid=(B, n_heads, nq),
        in_specs=[pl.BlockSpec((None, t, HEAD_DIM), lambda b, h, i, js: (b, i, h)),
                  pl.BlockSpec((None, S, HEAD_DIM), lambda b, h, i, js: (b, 0, n_heads + h)),
                  pl.BlockSpec((None, S, HEAD_DIM), lambda b, h, i, js: (b, 0, 2 * n_heads + h)),
                  pl.BlockSpec((None, None, 1, S), lambda b, h, i, js: (b, h, 0, 0))],
        out_specs=pl.BlockSpec((None, t, HEAD_DIM), lambda b, h, i, js: (b, i, h)),
        scratch_shapes=[pltpu.VMEM((t, 1), F32), pltpu.VMEM((t, 1), F32),
                        pltpu.VMEM((t, HEAD_DIM), F32)],
    )
    return pl.pallas_call(
        functools.partial(_fox_kernel, n_heads=n_heads),
        grid_spec=grid_spec,
        out_shape=jax.ShapeDtypeStruct((B, S, D), BF16),
        compiler_params=_params("parallel", "parallel", "arbitrary"),
        name="fox_attention",
    )(first, proj, proj, proj, fk)


def _sb_kernel(q_ref, k_ref, v_ref, o_ref, carry_ref, acc_ref):
    t = q_ref.shape[0]
    i = pl.program_id(2)
    carry_ref[...] = jnp.zeros_like(carry_ref)
    acc_ref[...] = jnp.zeros_like(acc_ref)
    q = q_ref[...]
    row = lax.broadcasted_iota(I32, (t, t), 0)
    col = lax.broadcasted_iota(I32, (t, t), 1)
    later = jnp.where(row > col, 1.0, 0.0).astype(BF16)

    def block(j, masked):
        ks = pl.multiple_of(j * t, t)
        k = k_ref[pl.ds(ks, t), :]
        v = v_ref[pl.ds(ks, t), :]
        z = lax.dot_general(q, k, (((1,), (1,)), ((), ())), preferred_element_type=F32)
        log1m = _log_sigmoid(-z)
        if masked:
            log1m = jnp.where(col < row, log1m, 0.0)
        hi, mid, lo = _split3(log1m)
        suffix = (jnp.dot(hi, later, preferred_element_type=F32)
                  + jnp.dot(mid, later, preferred_element_type=F32)
                  + jnp.dot(lo, later, preferred_element_type=F32))
        la = z + log1m + suffix + carry_ref[...]
        if masked:
            la = jnp.where(col < row, la, -jnp.inf)
        a = jnp.exp(la)
        acc_ref[...] += jnp.dot(a.astype(BF16), v, preferred_element_type=F32)
        carry_ref[...] += jnp.sum(log1m, axis=1, keepdims=True)

    block(i, True)

    def cond(state):
        j, top = state
        return jnp.logical_and(j >= 0, top > -UNDERFLOW_LOG)

    def body(state):
        j, _ = state
        block(j, False)
        return j - 1, jnp.max(carry_ref[...])

    lax.while_loop(cond, body, (i - 1, jnp.max(carry_ref[...])))
    o_ref[...] = acc_ref[...].astype(o_ref.dtype)


def _sb_attention(proj, n_heads):
    B, S, _ = proj.shape
    D = n_heads * HEAD_DIM
    t = min(SB_TILE, S)
    return pl.pallas_call(
        _sb_kernel,
        grid=(B, n_heads, S // t),
        in_specs=[pl.BlockSpec((None, t, HEAD_DIM), lambda b, h, i: (b, i, h)),
                  pl.BlockSpec((None, S, HEAD_DIM), lambda b, h, i: (b, 0, n_heads + h)),
                  pl.BlockSpec((None, S, HEAD_DIM), lambda b, h, i: (b, 0, 2 * n_heads + h))],
        out_specs=pl.BlockSpec((None, t, HEAD_DIM), lambda b, h, i: (b, i, h)),
        out_shape=jax.ShapeDtypeStruct((B, S, D), BF16),
        scratch_shapes=[pltpu.VMEM((t, 1), F32), pltpu.VMEM((t, HEAD_DIM), F32)],
        compiler_params=_params("parallel", "parallel", "arbitrary"),
        name="sb_attention",
    )(proj, proj, proj)


def _outproj_ln_kernel(o_ref, w_ref, x_ref, gate_ref, g_ref, b_ref, out_ref):
    y = jnp.dot(o_ref[...], w_ref[...], preferred_element_type=F32)
    z = DEEPNORM_ALPHA * x_ref[...] + gate_ref[...] * y
    out_ref[...] = _layer_norm(z, g_ref[...], b_ref[...])


def _out_projection_norm(o, w_bf16, x, gate, ln_g, ln_b):
    B, S, D = x.shape
    tm = min(ROW_TILE, S)
    row_spec = pl.BlockSpec((None, tm, D), lambda b, i: (b, i, 0))
    return pl.pallas_call(
        _outproj_ln_kernel,
        grid=(B, S // tm),
        in_specs=[row_spec,
                  pl.BlockSpec((D, D), lambda b, i: (0, 0)),
                  row_spec,
                  pl.BlockSpec((None, 1, D), lambda b, i: (b, 0, 0)),
                  pl.BlockSpec((1, D), lambda b, i: (0, 0)),
                  pl.BlockSpec((1, D), lambda b, i: (0, 0))],
        out_specs=row_spec,
        out_shape=jax.ShapeDtypeStruct((B, S, D), F32),
        compiler_params=_params("parallel", "parallel"),
        name="out_projection_norm",
    )(o, w_bf16, x, gate.reshape(B, 1, D), ln_g.reshape(1, D), ln_b.reshape(1, D))


def _router_kernel(x_ref, sh_ref, sc_ref, rw_ref, rb_ref,
                   h_ref, idx_ref, wts_ref, pos_ref, cnt_ref, carry_ref, *, n_experts):
    tm = x_ref.shape[0]

    @pl.when(jnp.logical_and(pl.program_id(0) == 0, pl.program_id(1) == 0))
    def _():
        carry_ref[...] = jnp.zeros_like(carry_ref)

    h = x_ref[...] * (1.0 + sc_ref[...]) + sh_ref[...]
    h_ref[...] = _pack_halves(h)
    logits = jnp.dot(h, rw_ref[...], preferred_element_type=F32, precision=lax.Precision.HIGHEST)
    scores = jax.nn.sigmoid(logits)
    lane = lax.broadcasted_iota(I32, (tm, LANES), 1).astype(F32)
    sel = jnp.where(lane < n_experts, scores + rb_ref[...], -jnp.inf)

    chosen = jnp.zeros((tm, LANES), F32)
    idx_acc = jnp.zeros((tm, LANES), F32)
    w_acc = jnp.zeros((tm, LANES), F32)
    total = jnp.zeros((tm, 1), F32)
    picks = []
    for k in range(TOP_K):
        best = jnp.max(sel, axis=1, keepdims=True)
        pick = jnp.min(jnp.where(sel == best, lane, float(LANES)), axis=1, keepdims=True)
        hit = lane == pick
        score = jnp.sum(jnp.where(hit, scores, 0.0), axis=1, keepdims=True)
        sel = jnp.where(hit, -jnp.inf, sel)
        chosen = jnp.where(hit, 1.0, chosen)
        idx_acc = jnp.where(lane == k, pick, idx_acc)
        w_acc = jnp.where(lane == k, score, w_acc)
        total = total + score
        picks.append(pick)
    w_acc = w_acc / total * ROUTED_SCALE

    row = lax.broadcasted_iota(I32, (tm, tm), 0)
    col = lax.broadcasted_iota(I32, (tm, tm), 1)
    earlier = jnp.where(row > col, 1.0, 0.0).astype(BF16)
    before = jnp.dot(earlier, chosen.astype(BF16), preferred_element_type=F32) + carry_ref[...]
    pos_acc = jnp.zeros((tm, LANES), F32)
    for k in range(TOP_K):
        rank = jnp.sum(jnp.where(lane == picks[k], before, 0.0), axis=1, keepdims=True)
        pos_acc = jnp.where(lane == k, rank, pos_acc)

    carry_ref[...] += jnp.sum(chosen, axis=0, keepdims=True)
    idx_ref[...] = idx_acc[:, :TOP_K].astype(I32)
    wts_ref[...] = w_acc[:, :TOP_K]
    pos_ref[...] = pos_acc[:, :TOP_K].astype(I32)
    cnt_ref[...] = carry_ref[...]


def _router(x, shift, scale, router_w, router_bias):
    B, S, D = x.shape
    E = router_w.shape[1]
    tm = min(ROW_TILE, S)
    rw = jnp.zeros((D, LANES), F32).at[:, :E].set(router_w)
    rb = jnp.zeros((1, LANES), F32).at[0, :E].set(router_bias)
    row_spec = pl.BlockSpec((None, tm, D), lambda b, i: (b, i, 0))
    vec_spec = pl.BlockSpec((None, 1, D), lambda b, i: (b, 0, 0))
    k_spec = pl.BlockSpec((None, tm, TOP_K), lambda b, i: (b, i, 0))
    return pl.pallas_call(
        functools.partial(_router_kernel, n_experts=E),
        grid=(B, S // tm),
        in_specs=[row_spec, vec_spec, vec_spec,
                  pl.BlockSpec((D, LANES), lambda b, i: (0, 0)),
                  pl.BlockSpec((1, LANES), lambda b, i: (0, 0))],
        out_specs=[pl.BlockSpec((None, tm, D // 2), lambda b, i: (b, i, 0)), k_spec, k_spec, k_spec,
                   pl.BlockSpec((1, LANES), lambda b, i: (0, 0))],
        out_shape=[jax.ShapeDtypeStruct((B, S, D // 2), U32),
                   jax.ShapeDtypeStruct((B, S, TOP_K), I32),
                   jax.ShapeDtypeStruct((B, S, TOP_K), F32),
                   jax.ShapeDtypeStruct((B, S, TOP_K), I32),
                   jax.ShapeDtypeStruct((1, LANES), F32)],
        scratch_shapes=[pltpu.VMEM((1, LANES), F32)],
        compiler_params=_params("arbitrary", "arbitrary"),
        name="router",
    )(x, shift.reshape(B, 1, D), scale.reshape(B, 1, D), rw, rb)


def _sc_mesh():
    return plsc.VectorSubcoreMesh(core_axis_name="core", subcore_axis_name="subcore")


def _sc_dispatch(h, dest, n_rows):
    T, D = h.shape
    K = dest.shape[1]
    W = SC_WINDOW
    per_worker = T // SC_WORKERS
    dest_w = dest.reshape(T // W, W, K).transpose(0, 2, 1).reshape(T * K)

    @pl.kernel(out_type=jax.ShapeDtypeStruct((n_rows, D), h.dtype), mesh=_sc_mesh(),
               scratch_types=[pltpu.VMEM((K * W,), I32), pltpu.VMEM((W, D), h.dtype)])
    def dispatch(h_hbm, idx_hbm, rows_hbm, idx_v, buf_v):
        worker = lax.axis_index("core") * SC_SUBCORES + lax.axis_index("subcore")

        @pl.loop(0, per_worker // W)
        def _(s):
            base = pl.multiple_of(worker * per_worker + s * W, W)
            pltpu.sync_copy(idx_hbm.at[pl.ds(base * K, K * W)], idx_v)
            pltpu.sync_copy(h_hbm.at[pl.ds(base, W)], buf_v)
            for k in range(K):
                pltpu.sync_copy(buf_v, rows_hbm.at[idx_v.at[pl.ds(k * W, W)]])

    return dispatch(h, dest_w)


def _sc_gather(rows, index):
    R = index.shape[0]
    D = rows.shape[1]
    W = SC_WINDOW
    per_worker = R // SC_WORKERS

    @pl.kernel(out_type=jax.ShapeDtypeStruct((R, D), rows.dtype), mesh=_sc_mesh(),
               scratch_types=[pltpu.VMEM((W,), I32), pltpu.VMEM((W, D), rows.dtype)])
    def gather(rows_hbm, idx_hbm, out_hbm, idx_v, buf_v):
        worker = lax.axis_index("core") * SC_SUBCORES + lax.axis_index("subcore")

        @pl.loop(0, per_worker // W)
        def _(s):
            base = pl.multiple_of(worker * per_worker + s * W, W)
            pltpu.sync_copy(idx_hbm.at[pl.ds(base, W)], idx_v)
            pltpu.sync_copy(rows_hbm.at[idx_v], buf_v)
            pltpu.sync_copy(buf_v, out_hbm.at[pl.ds(base, W)])

    return gather(rows, index)


def _swiglu_packed(x_packed, w_gate_up, w_down):
    half = x_packed.shape[1]
    f = w_down.shape[0]
    lo, hi = _unpack_halves(x_packed)
    gu = (jnp.dot(lo, w_gate_up[:half, :], preferred_element_type=F32)
          + jnp.dot(hi, w_gate_up[half:, :], preferred_element_type=F32))
    g = gu[:, :f]
    a = (g * jax.nn.sigmoid(g) * gu[:, f:]).astype(BF16)
    return jnp.dot(a, w_down, preferred_element_type=F32)


def _expert_kernel(be_ref, used_ref, x_ref, wgu_ref, wd_ref, o_ref):
    @pl.when(pl.program_id(0) < used_ref[0])
    def _():
        o_ref[...] = _pack_halves(_swiglu_packed(x_ref[...], wgu_ref[...], wd_ref[...]))


def _experts(rows, block_expert, n_used, w_gate_up, w_down):
    R, half = rows.shape
    D = 2 * half
    F2 = w_gate_up.shape[2]
    blk = EXPERT_BLOCK
    grid_spec = pltpu.PrefetchScalarGridSpec(
        num_scalar_prefetch=2,
        grid=(R // blk,),
        in_specs=[pl.BlockSpec((blk, half), lambda r, be, nu: (r, 0)),
                  pl.BlockSpec((None, D, F2), lambda r, be, nu: (be[r], 0, 0)),
                  pl.BlockSpec((None, F2 // 2, D), lambda r, be, nu: (be[r], 0, 0))],
        out_specs=pl.BlockSpec((blk, half), lambda r, be, nu: (r, 0)),
    )
    return pl.pallas_call(
        _expert_kernel,
        grid_spec=grid_spec,
        out_shape=jax.ShapeDtypeStruct((R, half), U32),
        compiler_params=_params("arbitrary"),
        name="routed_experts",
    )(block_expert, n_used, rows, w_gate_up, w_down)


def _moe_out_kernel(ga_ref, w_ref, h_ref, sgu_ref, sd_ref, x_ref, gate_ref, g_ref, b_ref, out_ref):
    half = h_ref.shape[1]
    y = _swiglu_packed(h_ref[...], sgu_ref[...], sd_ref[...])
    w = w_ref[...]
    y_lo = y[:, :half]
    y_hi = y[:, half:]
    for k in range(TOP_K):
        packed = ga_ref[:, k * half:(k + 1) * half]
        wk = w[:, k:k + 1]
        y_lo = y_lo + wk * lax.bitcast_convert_type(packed << 16, F32)
        y_hi = y_hi + wk * lax.bitcast_convert_type(packed & jnp.uint32(0xFFFF0000), F32)
    y = jnp.concatenate([y_lo, y_hi], axis=1)
    z = DEEPNORM_ALPHA * x_ref[...] + gate_ref[...] * y
    out_ref[...] = _layer_norm(z, g_ref[...], b_ref[...])


def _moe_output_norm(gathered, wts, h, s_gate_up, s_down, x, gate, ln_g, ln_b):
    B, S, D = x.shape
    half = D // 2
    F2 = s_gate_up.shape[1]
    tm = min(ROW_TILE // 2, S)
    row_spec = pl.BlockSpec((None, tm, D), lambda b, i: (b, i, 0))
    return pl.pallas_call(
        _moe_out_kernel,
        grid=(B, S // tm),
        in_specs=[pl.BlockSpec((None, tm, TOP_K * half), lambda b, i: (b, i, 0)),
                  pl.BlockSpec((None, tm, TOP_K), lambda b, i: (b, i, 0)),
                  pl.BlockSpec((None, tm, half), lambda b, i: (b, i, 0)),
                  pl.BlockSpec((D, F2), lambda b, i: (0, 0)),
                  pl.BlockSpec((F2 // 2, D), lambda b, i: (0, 0)),
                  row_spec,
                  pl.BlockSpec((None, 1, D), lambda b, i: (b, 0, 0)),
                  pl.BlockSpec((1, D), lambda b, i: (0, 0)),
                  pl.BlockSpec((1, D), lambda b, i: (0, 0))],
        out_specs=row_spec,
        out_shape=jax.ShapeDtypeStruct((B, S, D), F32),
        compiler_params=_params("parallel", "parallel"),
        name="moe_output_norm",
    )(gathered.reshape(B, S, TOP_K * half), wts, h, s_gate_up, s_down, x,
      gate.reshape(B, 1, D), ln_g.reshape(1, D), ln_b.reshape(1, D))


def _moe_layer(x, shift, scale, gate, router_w, router_bias, w_gate, w_up, w_down,
               s_gate, s_up, s_down, ln_g, ln_b):
    B, S, D = x.shape
    T = B * S
    E = router_w.shape[1]
    blk = EXPERT_BLOCK
    h, idx, wts, pos, cnt = _router(x, shift, scale, router_w, router_bias)

    counts = cnt[0, :E].astype(I32)
    padded = (counts + blk - 1) // blk * blk
    pend = jnp.cumsum(padded)
    pstart = pend - padded
    n_blocks = -(-T * TOP_K // blk) + E
    onehot = idx[..., None] == jnp.arange(E, dtype=I32)
    dest = jnp.sum(jnp.where(onehot, pstart, 0), axis=-1) + pos
    block_expert = jnp.minimum(
        jnp.sum((jnp.arange(n_blocks, dtype=I32)[:, None] * blk >= pend[None, :]).astype(I32), axis=1),
        E - 1).astype(I32)
    n_used = (pend[-1:] // blk).astype(I32)

    dest = dest.reshape(T, TOP_K)
    rows = _sc_dispatch(h.reshape(T, D // 2), dest, n_blocks * blk)
    w_gate_up = jnp.concatenate([w_gate, w_up], axis=-1).astype(BF16)
    out_rows = _experts(rows, block_expert, n_used, w_gate_up, w_down.astype(BF16))
    gathered = _sc_gather(out_rows, dest.reshape(T * TOP_K))
    s_gate_up = jnp.concatenate([s_gate, s_up], axis=-1).astype(BF16)
    return _moe_output_norm(gathered, wts, h, s_gate_up, s_down.astype(BF16), x, gate, ln_g, ln_b)


def kernel(x, c, fox_w_in, fox_b_f, fox_w_o, sb_w_in, sb_w_o, ada_w, ada_b, ln_attn_g, ln_attn_b,
           router_w, router_bias, exp_w_gate, exp_w_up, exp_w_down,
           shared_w_gate, shared_w_up, shared_w_down, ln_ffn_g, ln_ffn_b):
    B, S, D = x.shape
    H = D // HEAD_DIM
    depth = ada_w.shape[0]
    mod = _ada_mod(c, ada_w, ada_b)
    col_scale = jnp.concatenate([jnp.full((D,), ATTN_SCALE, F32), jnp.ones((2 * D,), F32)])
    for i in range(depth):
        sh_a, sc_a, g_a, sh_m, sc_m, g_m = jnp.split(mod[i], 6, axis=-1)
        j = i // 2
        if i % 2 == 0:
            w_in = fox_w_in[j]
            w_qkv = (w_in[:, :3 * D] * col_scale).astype(BF16)
            wf = jnp.zeros((D, LANES), F32).at[:, :H].set(w_in[:, 3 * D:]).astype(BF16)
            bf = jnp.zeros((1, LANES), F32).at[0, :H].set(fox_b_f[j])
            proj, fcum = _in_projection(x, sh_a, sc_a, w_qkv, forget=(wf, bf))
            o = _fox_attention(proj, fcum, H)
            w_o = fox_w_o[j]
        else:
            w_qkv = (sb_w_in[j] * col_scale).astype(BF16)
            proj = _in_projection(x, sh_a, sc_a, w_qkv)
            o = _sb_attention(proj, H)
            w_o = sb_w_o[j]
        x = _out_projection_norm(o, w_o.astype(BF16), x, 1.0 + g_a, ln_attn_g[i], ln_attn_b[i])
        x = _moe_layer(x, sh_m, sc_m, 1.0 + g_m, router_w[i], router_bias[i],
                       exp_w_gate[i], exp_w_up[i], exp_w_down[i],
                       shared_w_gate[i], shared_w_up[i], shared_w_down[i], ln_ffn_g[i], ln_ffn_b[i])
    return x
```

```python
import functools
import math

import jax
import jax.numpy as jnp
from jax import lax
from jax.experimental import pallas as pl
from jax.experimental.pallas import tpu as pltpu
from jax.experimental.pallas import tpu_sc as plsc

F32 = jnp.float32
BF16 = jnp.bfloat16
I32 = jnp.int32
U32 = jnp.uint32

HEAD_DIM = 128
TOP_K = 8
ROUTED_SCALE = 2.5
LN_EPS = 1e-5
DEPTH = 2
DEEPNORM_ALPHA = (2 * DEPTH) ** 0.25
ATTN_SCALE = 1.0 / math.sqrt(HEAD_DIM)

LANES = 128
UNDERFLOW_LOG = 105.0
VMEM_LIMIT = 56 * 1024 * 1024

ROW_TILE = 512
ATTN_TILE = 512
SB_TILE = 256
SB_HEADS_PER_STEP = 2
EXPERT_BLOCK = 512
SC_WINDOW = 64
SC_SUBCORES = 16
SC_WORKERS = 2 * SC_SUBCORES
N_BIAS_COLS = 3

NT_DIMS = (((1,), (1,)), ((), ()))


def _params(*sem):
    return pltpu.CompilerParams(dimension_semantics=sem, vmem_limit_bytes=VMEM_LIMIT)


def _log_sigmoid(z):
    return jnp.minimum(z, 0.0) - jnp.log1p(jnp.exp(-jnp.abs(z)))


def _split3(x):
    hi = x.astype(BF16)
    r = x - hi.astype(F32)
    mid = r.astype(BF16)
    lo = (r - mid.astype(F32)).astype(BF16)
    return hi, mid, lo


def _pack_halves(x):
    n = x.shape[1] // 2
    bits = lax.bitcast_convert_type(x.astype(BF16).astype(F32), U32)
    return (bits[:, :n] >> 16) | (bits[:, n:] & jnp.uint32(0xFFFF0000))


def _unpack_halves(w):
    lo = lax.bitcast_convert_type(w << 16, F32).astype(BF16)
    hi = lax.bitcast_convert_type(w & jnp.uint32(0xFFFF0000), F32).astype(BF16)
    return lo, hi


def _layer_norm(z, g, b):
    mu = jnp.mean(z, axis=-1, keepdims=True)
    d = z - mu
    var = jnp.mean(d * d, axis=-1, keepdims=True)
    return d * lax.rsqrt(var + LN_EPS) * g + b


def _ada_kernel(c_ref, w_ref, b_ref, o_ref):
    c = c_ref[...]
    act = c * jax.nn.sigmoid(c)
    o_ref[...] = jnp.dot(act, w_ref[...], preferred_element_type=F32,
                         precision=lax.Precision.HIGHEST) + b_ref[...]


def _ada_mod(c, ada_w, ada_b):
    B, D = c.shape
    L, _, N = ada_w.shape
    cp = jnp.zeros((8, D), F32).at[:B].set(c)
    out = pl.pallas_call(
        _ada_kernel,
        grid=(L, N // D),
        in_specs=[pl.BlockSpec((8, D), lambda l, j: (0, 0)),
                  pl.BlockSpec((None, D, D), lambda l, j: (l, 0, j)),
                  pl.BlockSpec((None, 1, D), lambda l, j: (l, 0, j))],
        out_specs=pl.BlockSpec((None, 8, D), lambda l, j: (l, 0, j)),
        out_shape=jax.ShapeDtypeStruct((L, 8, N), F32),
        compiler_params=_params("parallel", "parallel"),
        name="ada_mod",
    )(cp, ada_w, ada_b.reshape(L, 1, N))
    return out[:, :B]


def _project_qkv(h, wq_ref, wk_ref, wvt_ref, q_ref, k_ref, vt_ref):
    q_ref[...] = jnp.dot(h, wq_ref[...], preferred_element_type=F32).astype(BF16)
    k_ref[...] = jnp.dot(h, wk_ref[...], preferred_element_type=F32).astype(BF16)
    vt_ref[...] = lax.dot_general(wvt_ref[...], h, NT_DIMS, preferred_element_type=F32).astype(BF16)


def _inproj_kernel(x_ref, sh_ref, sc_ref, wq_ref, wk_ref, wvt_ref, q_ref, k_ref, vt_ref):
    h = (x_ref[...] * (1.0 + sc_ref[...]) + sh_ref[...]).astype(BF16)
    _project_qkv(h, wq_ref, wk_ref, wvt_ref, q_ref, k_ref, vt_ref)


def _inproj_forget_kernel(x_ref, sh_ref, sc_ref, wq_ref, wk_ref, wvt_ref, wf_ref, bf_ref,
                          q_ref, k_ref, vt_ref, f_ref, carry_ref):
    tm = x_ref.shape[0]
    h = (x_ref[...] * (1.0 + sc_ref[...]) + sh_ref[...]).astype(BF16)
    _project_qkv(h, wq_ref, wk_ref, wvt_ref, q_ref, k_ref, vt_ref)

    @pl.when(pl.program_id(1) == 0)
    def _():
        carry_ref[...] = jnp.zeros_like(carry_ref)

    logf = _log_sigmoid(jnp.dot(h, wf_ref[...], preferred_element_type=F32) + bf_ref[...])
    row = lax.broadcasted_iota(I32, (tm, tm), 0)
    col = lax.broadcasted_iota(I32, (tm, tm), 1)
    tri = jnp.where(row >= col, 1.0, 0.0).astype(BF16)
    hi, mid, lo = _split3(logf)
    cs = (jnp.dot(tri, hi, preferred_element_type=F32)
          + jnp.dot(tri, mid, preferred_element_type=F32)
          + jnp.dot(tri, lo, preferred_element_type=F32))
    cum = cs + carry_ref[...]
    f_ref[...] = cum
    carry_ref[...] = cum[tm - 1:tm, :]


def _in_projection(x, shift, scale, wq, wk, wvt, forget=None):
    B, S, D = x.shape
    tm = min(ROW_TILE, S)
    row_spec = pl.BlockSpec((None, tm, D), lambda b, i: (b, i, 0))
    vec_spec = pl.BlockSpec((None, 1, D), lambda b, i: (b, 0, 0))
    w_spec = pl.BlockSpec((D, D), lambda b, i: (0, 0))
    vt_spec = pl.BlockSpec((None, D, tm), lambda b, i: (b, 0, i))
    qkv_shapes = [jax.ShapeDtypeStruct((B, S, D), BF16), jax.ShapeDtypeStruct((B, S, D), BF16),
                  jax.ShapeDtypeStruct((B, D, S), BF16)]
    sh = shift.reshape(B, 1, D)
    sc = scale.reshape(B, 1, D)
    if forget is None:
        return pl.pallas_call(
            _inproj_kernel,
            grid=(B, S // tm),
            in_specs=[row_spec, vec_spec, vec_spec, w_spec, w_spec, w_spec],
            out_specs=[row_spec, row_spec, vt_spec],
            out_shape=qkv_shapes,
            compiler_params=_params("parallel", "parallel"),
            name="in_projection",
        )(x, sh, sc, wq, wk, wvt)
    wf, bf = forget
    return pl.pallas_call(
        _inproj_forget_kernel,
        grid=(B, S // tm),
        in_specs=[row_spec, vec_spec, vec_spec, w_spec, w_spec, w_spec,
                  pl.BlockSpec((D, LANES), lambda b, i: (0, 0)),
                  pl.BlockSpec((1, LANES), lambda b, i: (0, 0))],
        out_specs=[row_spec, row_spec, vt_spec, pl.BlockSpec((None, tm, LANES), lambda b, i: (b, i, 0))],
        out_shape=qkv_shapes + [jax.ShapeDtypeStruct((B, S, LANES), F32)],
        scratch_shapes=[pltpu.VMEM((1, LANES), F32)],
        compiler_params=_params("parallel", "arbitrary"),
        name="in_projection_forget",
    )(x, sh, sc, wq, wk, wvt, wf, bf)


def _fox_kernel(js_ref, q_ref, ka_ref, vt_ref, o_ref, m_ref, l_ref, acc_ref, *, n_heads):
    t = q_ref.shape[0]
    b, h, i = pl.program_id(0), pl.program_id(1), pl.program_id(2)
    nq = pl.num_programs(2)
    m_ref[...] = jnp.full(m_ref.shape, -jnp.inf, F32)
    l_ref[...] = jnp.zeros_like(l_ref)
    acc_ref[...] = jnp.zeros_like(acc_ref)
    sub = lax.broadcasted_iota(I32, (LANES, t), 0)
    q_aug_t = jnp.concatenate([q_ref[...].astype(F32).T.astype(BF16),
                               jnp.where(sub < N_BIAS_COLS, 1.0, 0.0).astype(BF16)], axis=0)

    def block(j, masked):
        ks = pl.multiple_of(j * t, t)
        s = jnp.dot(ka_ref[pl.ds(ks, t), :], q_aug_t, preferred_element_type=F32)
        if masked:
            key = lax.broadcasted_iota(I32, (t, t), 0)
            qry = lax.broadcasted_iota(I32, (t, t), 1)
            s = jnp.where(key <= qry, s, -jnp.inf)
        m_prev = m_ref[...]
        m_new = jnp.maximum(m_prev, jnp.max(s, axis=0, keepdims=True))
        alpha = jnp.exp(m_prev - m_new)
        p = jnp.exp(s - m_new)
        l_ref[...] = alpha * l_ref[...] + jnp.sum(p, axis=0, keepdims=True)
        acc_ref[...] = alpha * acc_ref[...] + jnp.dot(
            vt_ref[:, pl.ds(ks, t)], p.astype(BF16), preferred_element_type=F32)
        m_ref[...] = m_new

    def body(j, carry):
        block(j, False)
        return carry

    lax.fori_loop(js_ref[(b * n_heads + h) * nq + i], i, body, 0)
    block(i, True)
    o_ref[...] = (acc_ref[...] / l_ref[...]).T.astype(o_ref.dtype)


def _fox_first_blocks(q, k, fcum, n_heads, t):
    B, S, _ = q.shape
    q = q.astype(F32).reshape(B, S, n_heads, HEAD_DIM)
    k = k.astype(F32).reshape(B, S, n_heads, HEAD_DIM)
    qn = jnp.sqrt(jnp.sum(q * q, axis=-1))
    kn = jnp.sqrt(jnp.max(jnp.sum(k * k, axis=-1), axis=1))
    diag = jnp.sum(q * k, axis=-1)
    f = fcum[..., :n_heads]
    need = qn * kn[:, None, :] * 1.001 - diag + f + (UNDERFLOW_LOG + 1.0)
    nq = S // t
    need = jnp.max(need.reshape(B, nq, t, n_heads), axis=2)
    f_end = f.reshape(B, nq, t, n_heads)[:, :, t - 1, :]
    skippable = f_end[:, None, :, :] > need[:, :, None, :]
    first = jnp.sum(skippable.astype(I32), axis=2)
    first = jnp.minimum(first, jnp.arange(nq, dtype=I32)[None, :, None])
    return first.transpose(0, 2, 1).reshape(-1)


def _fox_attention(q, k, vt, fcum, n_heads):
    B, S, D = q.shape
    t = min(ATTN_TILE, S)
    nq = S // t
    first = _fox_first_blocks(q, k, fcum, n_heads, t)
    neg_f = -fcum[..., :n_heads]
    hi = lax.reduce_precision(neg_f, exponent_bits=8, mantissa_bits=7)
    mid = lax.reduce_precision(neg_f - hi, exponent_bits=8, mantissa_bits=7)
    pieces = jnp.stack([hi, mid, neg_f - hi - mid], axis=-1).astype(BF16)
    k_aug = jnp.concatenate(
        [k.reshape(B, S, n_heads, HEAD_DIM), pieces,
         jnp.zeros((B, S, n_heads, LANES - N_BIAS_COLS), BF16)], axis=-1).reshape(B, S, n_heads * 2 * LANES)
    grid_spec = pltpu.PrefetchScalarGridSpec(
        num_scalar_prefetch=1,
        grid=(B, n_heads, nq),
        in_specs=[pl.BlockSpec((None, t, HEAD_DIM), lambda b, h, i, js: (b, i, h)),
                  pl.BlockSpec((None, S, 2 * LANES), lambda b, h, i, js: (b, 0, h)),
                  pl.BlockSpec((None, HEAD_DIM, S), lambda b, h, i, js: (b, h, 0))],
        out_specs=pl.BlockSpec((None, t, HEAD_DIM), lambda b, h, i, js: (b, i, h)),
        scratch_shapes=[pltpu.VMEM((1, t), F32), pltpu.VMEM((1, t), F32),
                        pltpu.VMEM((HEAD_DIM, t), F32)],
    )
    return pl.pallas_call(
        functools.partial(_fox_kernel, n_heads=n_heads),
        grid_spec=grid_spec,
        out_shape=jax.ShapeDtypeStruct((B, S, D), BF16),
        compiler_params=_params("parallel", "parallel", "arbitrary"),
        name="fox_attention",
    )(first, q, k_aug, vt)


def _sb_kernel(q_ref, k_ref, vt_ref, o_ref, carry_ref, acc_ref):
    t = q_ref.shape[0]
    n_local = q_ref.shape[1] // HEAD_DIM
    i = pl.program_id(2)
    carry_ref[...] = jnp.zeros_like(carry_ref)
    acc_ref[...] = jnp.zeros_like(acc_ref)
    key = lax.broadcasted_iota(I32, (t, t), 0)
    qry = lax.broadcasted_iota(I32, (t, t), 1)
    later = jnp.where(qry > key, 1.0, 0.0).astype(BF16)
    later2 = jnp.concatenate([later, later], axis=1)

    def block(j, masked):
        ks = pl.multiple_of(j * t, t)
        for g in range(n_local):
            feat = slice(g * HEAD_DIM, (g + 1) * HEAD_DIM)
            z = lax.dot_general(k_ref[pl.ds(ks, t), feat], q_ref[:, feat], NT_DIMS,
                                preferred_element_type=F32)
            log1m = _log_sigmoid(-z)
            if masked:
                log1m = jnp.where(key < qry, log1m, 0.0)
            hi = log1m.astype(BF16)
            mid = (log1m - hi.astype(F32)).astype(BF16)
            suffix = jnp.dot(later2, jnp.concatenate([hi, mid], axis=0), preferred_element_type=F32)
            la = z + log1m + suffix + carry_ref[g:g + 1, :]
            if masked:
                la = jnp.where(key < qry, la, -jnp.inf)
            a = jnp.exp(la)
            acc_ref[feat, :] += jnp.dot(vt_ref[feat, pl.ds(ks, t)], a.astype(BF16), preferred_element_type=F32)
            carry_ref[g:g + 1, :] += jnp.sum(log1m, axis=0, keepdims=True)

    block(i, True)

    def cond(state):
        j, top = state
        return jnp.logical_and(j >= 0, top > -UNDERFLOW_LOG)

    def body(state):
        j, _ = state
        block(j, False)
        return j - 1, jnp.max(carry_ref[...])

    lax.while_loop(cond, body, (i - 1, jnp.max(carry_ref[...])))
    for g in range(n_local):
        feat = slice(g * HEAD_DIM, (g + 1) * HEAD_DIM)
        o_ref[:, feat] = acc_ref[feat, :].T.astype(o_ref.dtype)


def _sb_attention(q, k, vt, n_heads):
    B, S, D = q.shape
    t = min(SB_TILE, S)
    width = SB_HEADS_PER_STEP * HEAD_DIM
    return pl.pallas_call(
        _sb_kernel,
        grid=(B, n_heads // SB_HEADS_PER_STEP, S // t),
        in_specs=[pl.BlockSpec((None, t, width), lambda b, h, i: (b, i, h)),
                  pl.BlockSpec((None, S, width), lambda b, h, i: (b, 0, h)),
                  pl.BlockSpec((None, width, S), lambda b, h, i: (b, h, 0))],
        out_specs=pl.BlockSpec((None, t, width), lambda b, h, i: (b, i, h)),
        out_shape=jax.ShapeDtypeStruct((B, S, D), BF16),
        scratch_shapes=[pltpu.VMEM((SB_HEADS_PER_STEP, t), F32), pltpu.VMEM((width, t), F32)],
        compiler_params=_params("parallel", "parallel", "arbitrary"),
        name="sb_attention",
    )(q, k, vt)


def _outproj_ln_kernel(o_ref, w_ref, x_ref, gate_ref, g_ref, b_ref, out_ref):
    y = jnp.dot(o_ref[...], w_ref[...], preferred_element_type=F32)
    z = DEEPNORM_ALPHA * x_ref[...] + gate_ref[...] * y
    out_ref[...] = _layer_norm(z, g_ref[...], b_ref[...])


def _out_projection_norm(o, w_bf16, x, gate, ln_g, ln_b):
    B, S, D = x.shape
    tm = min(ROW_TILE, S)
    row_spec = pl.BlockSpec((None, tm, D), lambda b, i: (b, i, 0))
    return pl.pallas_call(
        _outproj_ln_kernel,
        grid=(B, S // tm),
        in_specs=[row_spec,
                  pl.BlockSpec((D, D), lambda b, i: (0, 0)),
                  row_spec,
                  pl.BlockSpec((None, 1, D), lambda b, i: (b, 0, 0)),
                  pl.BlockSpec((1, D), lambda b, i: (0, 0)),
                  pl.BlockSpec((1, D), lambda b, i: (0, 0))],
        out_specs=row_spec,
        out_shape=jax.ShapeDtypeStruct((B, S, D), F32),
        compiler_params=_params("parallel", "parallel"),
        name="out_projection_norm",
    )(o, w_bf16, x, gate.reshape(B, 1, D), ln_g.reshape(1, D), ln_b.reshape(1, D))


def _router_kernel(x_ref, sh_ref, sc_ref, rw_ref, rb_ref,
                   h_ref, idx_ref, wts_ref, pos_ref, cnt_ref, carry_ref, *, n_experts):
    tm = x_ref.shape[0]

    @pl.when(jnp.logical_and(pl.program_id(0) == 0, pl.program_id(1) == 0))
    def _():
        carry_ref[...] = jnp.zeros_like(carry_ref)

    h = x_ref[...] * (1.0 + sc_ref[...]) + sh_ref[...]
    h_ref[...] = _pack_halves(h)
    logits = jnp.dot(h, rw_ref[...], preferred_element_type=F32, precision=lax.Precision.HIGHEST)
    scores = jax.nn.sigmoid(logits)
    lane = lax.broadcasted_iota(I32, (tm, LANES), 1).astype(F32)
    sel = jnp.where(lane < n_experts, scores + rb_ref[...], -jnp.inf)

    chosen = jnp.zeros((tm, LANES), F32)
    idx_acc = jnp.zeros((tm, LANES), F32)
    w_acc = jnp.zeros((tm, LANES), F32)
    total = jnp.zeros((tm, 1), F32)
    picks = []
    for k in range(TOP_K):
        best = jnp.max(sel, axis=1, keepdims=True)
        pick = jnp.min(jnp.where(sel == best, lane, float(LANES)), axis=1, keepdims=True)
        hit = lane == pick
        score = jnp.sum(jnp.where(hit, scores, 0.0), axis=1, keepdims=True)
        sel = jnp.where(hit, -jnp.inf, sel)
        chosen = jnp.where(hit, 1.0, chosen)
        idx_acc = jnp.where(lane == k, pick, idx_acc)
        w_acc = jnp.where(lane == k, score, w_acc)
        total = total + score
        picks.append(pick)
    w_acc = w_acc / total * ROUTED_SCALE

    row = lax.broadcasted_iota(I32, (tm, tm), 0)
    col = lax.broadcasted_iota(I32, (tm, tm), 1)
    earlier = jnp.where(row > col, 1.0, 0.0).astype(BF16)
    before = jnp.dot(earlier, chosen.astype(BF16), preferred_element_type=F32) + carry_ref[...]
    pos_acc = jnp.zeros((tm, LANES), F32)
    for k in range(TOP_K):
        rank = jnp.sum(jnp.where(lane == picks[k], before, 0.0), axis=1, keepdims=True)
        pos_acc = jnp.where(lane == k, rank, pos_acc)

    carry_ref[...] += jnp.sum(chosen, axis=0, keepdims=True)
    idx_ref[...] = idx_acc[:, :TOP_K].astype(I32)
    wts_ref[...] = w_acc[:, :TOP_K]
    pos_ref[...] = pos_acc[:, :TOP_K].astype(I32)
    cnt_ref[...] = carry_ref[...]


def _router(x, shift, scale, router_w, router_bias):
    B, S, D = x.shape
    E = router_w.shape[1]
    tm = min(ROW_TILE, S)
    rw = jnp.zeros((D, LANES), F32).at[:, :E].set(router_w)
    rb = jnp.zeros((1, LANES), F32).at[0, :E].set(router_bias)
    row_spec = pl.BlockSpec((None, tm, D), lambda b, i: (b, i, 0))
    vec_spec = pl.BlockSpec((None, 1, D), lambda b, i: (b, 0, 0))
    k_spec = pl.BlockSpec((None, tm, TOP_K), lambda b, i: (b, i, 0))
    return pl.pallas_call(
        functools.partial(_router_kernel, n_experts=E),
        grid=(B, S // tm),
        in_specs=[row_spec, vec_spec, vec_spec,
                  pl.BlockSpec((D, LANES), lambda b, i: (0, 0)),
                  pl.BlockSpec((1, LANES), lambda b, i: (0, 0))],
        out_specs=[pl.BlockSpec((None, tm, D // 2), lambda b, i: (b, i, 0)), k_spec, k_spec, k_spec,
                   pl.BlockSpec((1, LANES), lambda b, i: (0, 0))],
        out_shape=[jax.ShapeDtypeStruct((B, S, D // 2), U32),
                   jax.ShapeDtypeStruct((B, S, TOP_K), I32),
                   jax.ShapeDtypeStruct((B, S, TOP_K), F32),
                   jax.ShapeDtypeStruct((B, S, TOP_K), I32),
                   jax.ShapeDtypeStruct((1, LANES), F32)],
        scratch_shapes=[pltpu.VMEM((1, LANES), F32)],
        compiler_params=_params("arbitrary", "arbitrary"),
        name="router",
    )(x, shift.reshape(B, 1, D), scale.reshape(B, 1, D), rw, rb)


def _sc_mesh():
    return plsc.VectorSubcoreMesh(core_axis_name="core", subcore_axis_name="subcore")


def _sc_dispatch(h, dest, n_rows):
    T, D = h.shape
    K = dest.shape[1]
    W = SC_WINDOW
    per_worker = T // SC_WORKERS
    dest_w = dest.reshape(T // W, W, K).transpose(0, 2, 1).reshape(T * K)

    @pl.kernel(out_type=jax.ShapeDtypeStruct((n_rows, D), h.dtype), mesh=_sc_mesh(),
               scratch_types=[pltpu.VMEM((K * W,), I32), pltpu.VMEM((W, D), h.dtype)])
    def dispatch(h_hbm, idx_hbm, rows_hbm, idx_v, buf_v):
        worker = lax.axis_index("core") * SC_SUBCORES + lax.axis_index("subcore")

        @pl.loop(0, per_worker // W)
        def _(s):
            base = pl.multiple_of(worker * per_worker + s * W, W)
            pltpu.sync_copy(idx_hbm.at[pl.ds(base * K, K * W)], idx_v)
            pltpu.sync_copy(h_hbm.at[pl.ds(base, W)], buf_v)
            for k in range(K):
                pltpu.sync_copy(buf_v, rows_hbm.at[idx_v.at[pl.ds(k * W, W)]])

    return dispatch(h, dest_w)


def _sc_gather(rows, index):
    R = index.shape[0]
    D = rows.shape[1]
    W = SC_WINDOW
    per_worker = R // SC_WORKERS

    @pl.kernel(out_type=jax.ShapeDtypeStruct((R, D), rows.dtype), mesh=_sc_mesh(),
               scratch_types=[pltpu.VMEM((W,), I32), pltpu.VMEM((W, D), rows.dtype)])
    def gather(rows_hbm, idx_hbm, out_hbm, idx_v, buf_v):
        worker = lax.axis_index("core") * SC_SUBCORES + lax.axis_index("subcore")

        @pl.loop(0, per_worker // W)
        def _(s):
            base = pl.multiple_of(worker * per_worker + s * W, W)
            pltpu.sync_copy(idx_hbm.at[pl.ds(base, W)], idx_v)
            pltpu.sync_copy(rows_hbm.at[idx_v], buf_v)
            pltpu.sync_copy(buf_v, out_hbm.at[pl.ds(base, W)])

    return gather(rows, index)


def _swiglu_packed(x_packed, w_gate_up, w_down):
    half = x_packed.shape[1]
    f = w_down.shape[0]
    lo, hi = _unpack_halves(x_packed)
    gu = (jnp.dot(lo, w_gate_up[:half, :], preferred_element_type=F32)
          + jnp.dot(hi, w_gate_up[half:, :], preferred_element_type=F32))
    g = gu[:, :f]
    a = (g * jax.nn.sigmoid(g) * gu[:, f:]).astype(BF16)
    return jnp.dot(a, w_down, preferred_element_type=F32)


def _expert_kernel(be_ref, used_ref, x_ref, wgu_ref, wd_ref, o_ref):
    @pl.when(pl.program_id(0) < used_ref[0])
    def _():
        o_ref[...] = _pack_halves(_swiglu_packed(x_ref[...], wgu_ref[...], wd_ref[...]))


def _experts(rows, block_expert, n_used, w_gate_up, w_down):
    R, half = rows.shape
    D = 2 * half
    F2 = w_gate_up.shape[2]
    blk = EXPERT_BLOCK
    grid_spec = pltpu.PrefetchScalarGridSpec(
        num_scalar_prefetch=2,
        grid=(R // blk,),
        in_specs=[pl.BlockSpec((blk, half), lambda r, be, nu: (r, 0)),
                  pl.BlockSpec((None, D, F2), lambda r, be, nu: (be[r], 0, 0)),
                  pl.BlockSpec((None, F2 // 2, D), lambda r, be, nu: (be[r], 0, 0))],
        out_specs=pl.BlockSpec((blk, half), lambda r, be, nu: (r, 0)),
    )
    return pl.pallas_call(
        _expert_kernel,
        grid_spec=grid_spec,
        out_shape=jax.ShapeDtypeStruct((R, half), U32),
        compiler_params=_params("arbitrary"),
        name="routed_experts",
    )(block_expert, n_used, rows, w_gate_up, w_down)


def _moe_out_kernel(ga_ref, w_ref, h_ref, sgu_ref, sd_ref, x_ref, gate_ref, g_ref, b_ref, out_ref):
    half = h_ref.shape[1]
    y = _swiglu_packed(h_ref[...], sgu_ref[...], sd_ref[...])
    w = w_ref[...]
    y_lo = y[:, :half]
    y_hi = y[:, half:]
    for k in range(TOP_K):
        packed = ga_ref[k]
        wk = w[:, k:k + 1]
        y_lo = y_lo + wk * lax.bitcast_convert_type(packed << 16, F32)
        y_hi = y_hi + wk * lax.bitcast_convert_type(packed & jnp.uint32(0xFFFF0000), F32)
    y = jnp.concatenate([y_lo, y_hi], axis=1)
    z = DEEPNORM_ALPHA * x_ref[...] + gate_ref[...] * y
    out_ref[...] = _layer_norm(z, g_ref[...], b_ref[...])


def _moe_output_norm(gathered, wts, h, s_gate_up, s_down, x, gate, ln_g, ln_b):
    B, S, D = x.shape
    half = D // 2
    F2 = s_gate_up.shape[1]
    tm = min(ROW_TILE // 2, S)
    row_spec = pl.BlockSpec((None, tm, D), lambda b, i: (b, i, 0))
    return pl.pallas_call(
        _moe_out_kernel,
        grid=(B, S // tm),
        in_specs=[pl.BlockSpec((TOP_K, None, tm, half), lambda b, i: (0, b, i, 0)),
                  pl.BlockSpec((None, tm, TOP_K), lambda b, i: (b, i, 0)),
                  pl.BlockSpec((None, tm, half), lambda b, i: (b, i, 0)),
                  pl.BlockSpec((D, F2), lambda b, i: (0, 0)),
                  pl.BlockSpec((F2 // 2, D), lambda b, i: (0, 0)),
                  row_spec,
                  pl.BlockSpec((None, 1, D), lambda b, i: (b, 0, 0)),
                  pl.BlockSpec((1, D), lambda b, i: (0, 0)),
                  pl.BlockSpec((1, D), lambda b, i: (0, 0))],
        out_specs=row_spec,
        out_shape=jax.ShapeDtypeStruct((B, S, D), F32),
        compiler_params=_params("parallel", "parallel"),
        name="moe_output_norm",
    )(gathered.reshape(TOP_K, B, S, half), wts, h, s_gate_up, s_down, x,
      gate.reshape(B, 1, D), ln_g.reshape(1, D), ln_b.reshape(1, D))


def _moe_layer(x, shift, scale, gate, router_w, router_bias, w_gate, w_up, w_down,
               s_gate, s_up, s_down, ln_g, ln_b):
    B, S, D = x.shape
    T = B * S
    E = router_w.shape[1]
    blk = EXPERT_BLOCK
    h, idx, wts, pos, cnt = _router(x, shift, scale, router_w, router_bias)

    counts = cnt[0, :E].astype(I32)
    padded = (counts + blk - 1) // blk * blk
    pend = jnp.cumsum(padded)
    pstart = pend - padded
    n_blocks = -(-T * TOP_K // blk) + E
    onehot = idx[..., None] == jnp.arange(E, dtype=I32)
    dest = jnp.sum(jnp.where(onehot, pstart, 0), axis=-1) + pos
    block_expert = jnp.minimum(
        jnp.sum((jnp.arange(n_blocks, dtype=I32)[:, None] * blk >= pend[None, :]).astype(I32), axis=1),
        E - 1).astype(I32)
    n_used = (pend[-1:] // blk).astype(I32)

    dest = dest.reshape(T, TOP_K)
    rows = _sc_dispatch(h.reshape(T, D // 2), dest, n_blocks * blk)
    w_gate_up = jnp.concatenate([w_gate, w_up], axis=-1).astype(BF16)
    out_rows = _experts(rows, block_expert, n_used, w_gate_up, w_down.astype(BF16))
    gathered = _sc_gather(out_rows, dest.T.reshape(TOP_K * T))
    s_gate_up = jnp.concatenate([s_gate, s_up], axis=-1).astype(BF16)
    return _moe_output_norm(gathered, wts, h, s_gate_up, s_down.astype(BF16), x, gate, ln_g, ln_b)


def kernel(x, c, fox_w_in, fox_b_f, fox_w_o, sb_w_in, sb_w_o, ada_w, ada_b, ln_attn_g, ln_attn_b,
           router_w, router_bias, exp_w_gate, exp_w_up, exp_w_down,
           shared_w_gate, shared_w_up, shared_w_down, ln_ffn_g, ln_ffn_b):
    B, S, D = x.shape
    H = D // HEAD_DIM
    depth = ada_w.shape[0]
    mod = _ada_mod(c, ada_w, ada_b)
    for i in range(depth):
        sh_a, sc_a, g_a, sh_m, sc_m, g_m = jnp.split(mod[i], 6, axis=-1)
        j = i // 2
        w_in = fox_w_in[j] if i % 2 == 0 else sb_w_in[j]
        wq = (w_in[:, :D] * ATTN_SCALE).astype(BF16)
        wk = w_in[:, D:2 * D].astype(BF16)
        wvt = w_in[:, 2 * D:3 * D].T.astype(BF16)
        if i % 2 == 0:
            wf = jnp.zeros((D, LANES), F32).at[:, :H].set(w_in[:, 3 * D:]).astype(BF16)
            bf = jnp.zeros((1, LANES), F32).at[0, :H].set(fox_b_f[j])
            q, k, vt, fcum = _in_projection(x, sh_a, sc_a, wq, wk, wvt, forget=(wf, bf))
            o = _fox_attention(q, k, vt, fcum, H)
            w_o = fox_w_o[j]
        else:
            q, k, vt = _in_projection(x, sh_a, sc_a, wq, wk, wvt)
            o = _sb_attention(q, k, vt, H)
            w_o = sb_w_o[j]
        x = _out_projection_norm(o, w_o.astype(BF16), x, 1.0 + g_a, ln_attn_g[i], ln_attn_b[i])
        x = _moe_layer(x, sh_m, sc_m, 1.0 + g_m, router_w[i], router_bias[i],
                       exp_w_gate[i], exp_w_up[i], exp_w_down[i],
                       shared_w_gate[i], shared_w_up[i], shared_w_down[i], ln_ffn_g[i], ln_ffn_b[i])
    return x
```

```python
import functools
import math

import jax
import jax.numpy as jnp
from jax import lax
from jax.experimental import pallas as pl
from jax.experimental.pallas import tpu as pltpu
from jax.experimental.pallas import tpu_sc as plsc

F32 = jnp.float32
BF16 = jnp.bfloat16
I32 = jnp.int32
U32 = jnp.uint32

HEAD_DIM = 128
TOP_K = 8
ROUTED_SCALE = 2.5
LN_EPS = 1e-5
DEPTH = 2
DEEPNORM_ALPHA = (2 * DEPTH) ** 0.25
ATTN_SCALE = 1.0 / math.sqrt(HEAD_DIM)

LANES = 128
UNDERFLOW_LOG = 105.0
VMEM_LIMIT = 56 * 1024 * 1024

ROW_TILE = 512
ATTN_TILE = 512
SB_TILE = 256
SB_HEADS_PER_STEP = 2
EXPERT_BLOCK = 512
SC_WINDOW = 64
SC_SUBCORES = 16
SC_WORKERS = 2 * SC_SUBCORES
N_BIAS_COLS = 3
BOUND_SLACK = 1.008

NT_DIMS = (((1,), (1,)), ((), ()))


def _params(*sem):
    return pltpu.CompilerParams(dimension_semantics=sem, vmem_limit_bytes=VMEM_LIMIT)


def _log_sigmoid(z):
    return jnp.minimum(z, 0.0) - jnp.log1p(jnp.exp(-jnp.abs(z)))


def _split3(x):
    hi = x.astype(BF16)
    r = x - hi.astype(F32)
    mid = r.astype(BF16)
    lo = (r - mid.astype(F32)).astype(BF16)
    return hi, mid, lo


def _pack_halves(x):
    n = x.shape[1] // 2
    bits = lax.bitcast_convert_type(x.astype(BF16).astype(F32), U32)
    return (bits[:, :n] >> 16) | (bits[:, n:] & jnp.uint32(0xFFFF0000))


def _unpack_halves(w):
    lo = lax.bitcast_convert_type(w << 16, F32).astype(BF16)
    hi = lax.bitcast_convert_type(w & jnp.uint32(0xFFFF0000), F32).astype(BF16)
    return lo, hi


def _layer_norm(z, g, b):
    mu = jnp.mean(z, axis=-1, keepdims=True)
    d = z - mu
    var = jnp.mean(d * d, axis=-1, keepdims=True)
    return d * lax.rsqrt(var + LN_EPS) * g + b


def _ada_kernel(c_ref, w_ref, b_ref, o_ref):
    c = c_ref[...]
    act = c * jax.nn.sigmoid(c)
    o_ref[...] = jnp.dot(act, w_ref[...], preferred_element_type=F32,
                         precision=lax.Precision.HIGHEST) + b_ref[...]


def _ada_mod(c, ada_w, ada_b):
    B, D = c.shape
    L, _, N = ada_w.shape
    cp = jnp.zeros((8, D), F32).at[:B].set(c)
    out = pl.pallas_call(
        _ada_kernel,
        grid=(L, N // D),
        in_specs=[pl.BlockSpec((8, D), lambda l, j: (0, 0)),
                  pl.BlockSpec((None, D, D), lambda l, j: (l, 0, j)),
                  pl.BlockSpec((None, 1, D), lambda l, j: (l, 0, j))],
        out_specs=pl.BlockSpec((None, 8, D), lambda l, j: (l, 0, j)),
        out_shape=jax.ShapeDtypeStruct((L, 8, N), F32),
        compiler_params=_params("parallel", "parallel"),
        name="ada_mod",
    )(cp, ada_w, ada_b.reshape(L, 1, N))
    return out[:, :B]


def _project_qkv(h, wq_ref, wk_ref, wvt_ref, q_ref, k_ref, vt_ref):
    q_ref[...] = jnp.dot(h, wq_ref[...], preferred_element_type=F32).astype(BF16)
    k_ref[...] = jnp.dot(h, wk_ref[...], preferred_element_type=F32).astype(BF16)
    vt_ref[...] = lax.dot_general(wvt_ref[...], h, NT_DIMS, preferred_element_type=F32).astype(BF16)


def _inproj_kernel(x_ref, sh_ref, sc_ref, wq_ref, wk_ref, wvt_ref, q_ref, k_ref, vt_ref):
    h = (x_ref[...] * (1.0 + sc_ref[...]) + sh_ref[...]).astype(BF16)
    _project_qkv(h, wq_ref, wk_ref, wvt_ref, q_ref, k_ref, vt_ref)


def _inproj_forget_kernel(x_ref, sh_ref, sc_ref, wq_ref, wk_ref, wvt_ref, wf_ref, bf_ref, sel_ref, hsum_ref,
                          q_ref, ka_ref, vt_ref, f_ref, stats_ref, carry_ref):
    tm, D = x_ref.shape
    n_heads = D // HEAD_DIM
    h = (x_ref[...] * (1.0 + sc_ref[...]) + sh_ref[...]).astype(BF16)
    q = jnp.dot(h, wq_ref[...], preferred_element_type=F32).astype(BF16)
    k = jnp.dot(h, wk_ref[...], preferred_element_type=F32).astype(BF16)
    q_ref[...] = q
    vt_ref[...] = lax.dot_general(wvt_ref[...], h, NT_DIMS, preferred_element_type=F32).astype(BF16)

    @pl.when(pl.program_id(1) == 0)
    def _():
        carry_ref[...] = jnp.zeros_like(carry_ref)

    logf = _log_sigmoid(jnp.dot(h, wf_ref[...], preferred_element_type=F32) + bf_ref[...])
    row = lax.broadcasted_iota(I32, (tm, tm), 0)
    col = lax.broadcasted_iota(I32, (tm, tm), 1)
    tri = jnp.where(row >= col, 1.0, 0.0).astype(BF16)
    hi, mid, lo = _split3(logf)
    cs = (jnp.dot(tri, hi, preferred_element_type=F32)
          + jnp.dot(tri, mid, preferred_element_type=F32)
          + jnp.dot(tri, lo, preferred_element_type=F32))
    cum = cs + carry_ref[...]
    f_ref[...] = cum
    carry_ref[...] = cum[tm - 1:tm, :]

    pieces = jnp.concatenate(_split3(-cum), axis=1)
    for g in range(n_heads):
        ka_ref[:, 2 * g * LANES:(2 * g + 1) * LANES] = k[:, g * HEAD_DIM:(g + 1) * HEAD_DIM]
        ka_ref[:, (2 * g + 1) * LANES:(2 * g + 2) * LANES] = jnp.dot(
            pieces, sel_ref[g], preferred_element_type=F32).astype(BF16)

    qf = q.astype(F32)
    kf = k.astype(F32)
    stats_ref[...] = (jnp.dot((qf * qf).astype(BF16), hsum_ref[0], preferred_element_type=F32)
                      + jnp.dot((kf * kf).astype(BF16), hsum_ref[1], preferred_element_type=F32)
                      + jnp.dot((qf * kf).astype(BF16), hsum_ref[2], preferred_element_type=F32))


def _in_projection(x, shift, scale, wq, wk, wvt):
    B, S, D = x.shape
    tm = min(ROW_TILE, S)
    row_spec = pl.BlockSpec((None, tm, D), lambda b, i: (b, i, 0))
    vec_spec = pl.BlockSpec((None, 1, D), lambda b, i: (b, 0, 0))
    w_spec = pl.BlockSpec((D, D), lambda b, i: (0, 0))
    return pl.pallas_call(
        _inproj_kernel,
        grid=(B, S // tm),
        in_specs=[row_spec, vec_spec, vec_spec, w_spec, w_spec, w_spec],
        out_specs=[row_spec, row_spec, pl.BlockSpec((None, D, tm), lambda b, i: (b, 0, i))],
        out_shape=[jax.ShapeDtypeStruct((B, S, D), BF16), jax.ShapeDtypeStruct((B, S, D), BF16),
                   jax.ShapeDtypeStruct((B, D, S), BF16)],
        compiler_params=_params("parallel", "parallel"),
        name="in_projection",
    )(x, shift.reshape(B, 1, D), scale.reshape(B, 1, D), wq, wk, wvt)


def _in_projection_forget(x, shift, scale, wq, wk, wvt, wf, bf):
    B, S, D = x.shape
    n_heads = D // HEAD_DIM
    tm = min(ROW_TILE, S)
    row_spec = pl.BlockSpec((None, tm, D), lambda b, i: (b, i, 0))
    vec_spec = pl.BlockSpec((None, 1, D), lambda b, i: (b, 0, 0))
    w_spec = pl.BlockSpec((D, D), lambda b, i: (0, 0))
    lane_spec = pl.BlockSpec((None, tm, LANES), lambda b, i: (b, i, 0))
    heads = jnp.arange(n_heads)
    sel = jnp.zeros((n_heads, N_BIAS_COLS * LANES, LANES), BF16)
    for p in range(N_BIAS_COLS):
        sel = sel.at[heads, p * LANES + heads, p].set(1.0)
    feat = jnp.arange(D)
    hsum = jnp.zeros((3, D, LANES), BF16)
    for c in range(3):
        hsum = hsum.at[c, feat, c * n_heads + feat // HEAD_DIM].set(1.0)
    return pl.pallas_call(
        _inproj_forget_kernel,
        grid=(B, S // tm),
        in_specs=[row_spec, vec_spec, vec_spec, w_spec, w_spec, w_spec,
                  pl.BlockSpec((D, LANES), lambda b, i: (0, 0)),
                  pl.BlockSpec((1, LANES), lambda b, i: (0, 0)),
                  pl.BlockSpec((n_heads, N_BIAS_COLS * LANES, LANES), lambda b, i: (0, 0, 0)),
                  pl.BlockSpec((3, D, LANES), lambda b, i: (0, 0, 0))],
        out_specs=[row_spec, pl.BlockSpec((None, tm, 2 * D), lambda b, i: (b, i, 0)),
                   pl.BlockSpec((None, D, tm), lambda b, i: (b, 0, i)), lane_spec, lane_spec],
        out_shape=[jax.ShapeDtypeStruct((B, S, D), BF16), jax.ShapeDtypeStruct((B, S, 2 * D), BF16),
                   jax.ShapeDtypeStruct((B, D, S), BF16), jax.ShapeDtypeStruct((B, S, LANES), F32),
                   jax.ShapeDtypeStruct((B, S, LANES), F32)],
        scratch_shapes=[pltpu.VMEM((1, LANES), F32)],
        compiler_params=_params("parallel", "arbitrary"),
        name="in_projection_forget",
    )(x, shift.reshape(B, 1, D), scale.reshape(B, 1, D), wq, wk, wvt, wf, bf, sel, hsum)


def _fox_kernel(js_ref, q_ref, ka_ref, vt_ref, o_ref, m_ref, l_ref, acc_ref, *, n_heads):
    t = q_ref.shape[0]
    b, h, i = pl.program_id(0), pl.program_id(1), pl.program_id(2)
    nq = pl.num_programs(2)
    m_ref[...] = jnp.full(m_ref.shape, -jnp.inf, F32)
    l_ref[...] = jnp.zeros_like(l_ref)
    acc_ref[...] = jnp.zeros_like(acc_ref)
    sub = lax.broadcasted_iota(I32, (LANES, t), 0)
    q_aug_t = jnp.concatenate([q_ref[...].astype(F32).T.astype(BF16),
                               jnp.where(sub < N_BIAS_COLS, 1.0, 0.0).astype(BF16)], axis=0)

    def block(j, masked):
        ks = pl.multiple_of(j * t, t)
        s = jnp.dot(ka_ref[pl.ds(ks, t), :], q_aug_t, preferred_element_type=F32)
        if masked:
            key = lax.broadcasted_iota(I32, (t, t), 0)
            qry = lax.broadcasted_iota(I32, (t, t), 1)
            s = jnp.where(key <= qry, s, -jnp.inf)
        m_prev = m_ref[...]
        m_new = jnp.maximum(m_prev, jnp.max(s, axis=0, keepdims=True))
        alpha = jnp.exp(m_prev - m_new)
        p = jnp.exp(s - m_new)
        l_ref[...] = alpha * l_ref[...] + jnp.sum(p, axis=0, keepdims=True)
        acc_ref[...] = alpha * acc_ref[...] + jnp.dot(
            vt_ref[:, pl.ds(ks, t)], p.astype(BF16), preferred_element_type=F32)
        m_ref[...] = m_new

    def body(j, carry):
        block(j, False)
        return carry

    lax.fori_loop(js_ref[(b * n_heads + h) * nq + i], i, body, 0)
    block(i, True)
    o_ref[...] = (acc_ref[...] / l_ref[...]).T.astype(o_ref.dtype)


def _fox_first_blocks(stats, fcum, n_heads, t):
    B, S, _ = stats.shape
    qn = jnp.sqrt(stats[..., :n_heads])
    kn = jnp.sqrt(jnp.max(stats[..., n_heads:2 * n_heads], axis=1))
    diag = stats[..., 2 * n_heads:3 * n_heads]
    f = fcum[..., :n_heads]
    need = qn * kn[:, None, :] * BOUND_SLACK - diag + f + (UNDERFLOW_LOG + 1.0)
    nq = S // t
    need = jnp.max(need.reshape(B, nq, t, n_heads), axis=2)
    f_end = f.reshape(B, nq, t, n_heads)[:, :, t - 1, :]
    skippable = f_end[:, None, :, :] > need[:, :, None, :]
    first = jnp.sum(skippable.astype(I32), axis=2)
    first = jnp.minimum(first, jnp.arange(nq, dtype=I32)[None, :, None])
    return first.transpose(0, 2, 1).reshape(-1)


def _fox_attention(q, k_aug, vt, fcum, stats, n_heads):
    B, S, D = q.shape
    t = min(ATTN_TILE, S)
    nq = S // t
    first = _fox_first_blocks(stats, fcum, n_heads, t)
    grid_spec = pltpu.PrefetchScalarGridSpec(
        num_scalar_prefetch=1,
        grid=(B, n_heads, nq),
        in_specs=[pl.BlockSpec((None, t, HEAD_DIM), lambda b, h, i, js: (b, i, h)),
                  pl.BlockSpec((None, S, 2 * LANES), lambda b, h, i, js: (b, 0, h)),
                  pl.BlockSpec((None, HEAD_DIM, S), lambda b, h, i, js: (b, h, 0))],
        out_specs=pl.BlockSpec((None, t, HEAD_DIM), lambda b, h, i, js: (b, i, h)),
        scratch_shapes=[pltpu.VMEM((1, t), F32), pltpu.VMEM((1, t), F32),
                        pltpu.VMEM((HEAD_DIM, t), F32)],
    )
    return pl.pallas_call(
        functools.partial(_fox_kernel, n_heads=n_heads),
        grid_spec=grid_spec,
        out_shape=jax.ShapeDtypeStruct((B, S, D), BF16),
        compiler_params=_params("parallel", "parallel", "arbitrary"),
        name="fox_attention",
    )(first, q, k_aug, vt)


def _sb_kernel(q_ref, k_ref, vt_ref, o_ref, carry_ref, acc_ref):
    t = q_ref.shape[0]
    n_local = q_ref.shape[1] // HEAD_DIM
    i = pl.program_id(2)
    carry_ref[...] = jnp.zeros_like(carry_ref)
    acc_ref[...] = jnp.zeros_like(acc_ref)
    key = lax.broadcasted_iota(I32, (t, t), 0)
    qry = lax.broadcasted_iota(I32, (t, t), 1)
    later = jnp.where(qry > key, 1.0, 0.0).astype(BF16)
    later2 = jnp.concatenate([later, later], axis=1)

    def block(j, masked):
        ks = pl.multiple_of(j * t, t)
        for g in range(n_local):
            feat = slice(g * HEAD_DIM, (g + 1) * HEAD_DIM)
            z = lax.dot_general(k_ref[pl.ds(ks, t), feat], q_ref[:, feat], NT_DIMS,
                                preferred_element_type=F32)
            log1m = _log_sigmoid(-z)
            if masked:
                log1m = jnp.where(key < qry, log1m, 0.0)
            hi = log1m.astype(BF16)
            mid = (log1m - hi.astype(F32)).astype(BF16)
            suffix = jnp.dot(later2, jnp.concatenate([hi, mid], axis=0), preferred_element_type=F32)
            la = z + log1m + suffix + carry_ref[g:g + 1, :]
            if masked:
                la = jnp.where(key < qry, la, -jnp.inf)
            a = jnp.exp(la)
            acc_ref[feat, :] += jnp.dot(vt_ref[feat, pl.ds(ks, t)], a.astype(BF16), preferred_element_type=F32)
            carry_ref[g:g + 1, :] += jnp.sum(log1m, axis=0, keepdims=True)

    block(i, True)

    def cond(state):
        j, top = state
        return jnp.logical_and(j >= 0, top > -UNDERFLOW_LOG)

    def body(state):
        j, _ = state
        block(j, False)
        return j - 1, jnp.max(carry_ref[...])

    lax.while_loop(cond, body, (i - 1, jnp.max(carry_ref[...])))
    for g in range(n_local):
        feat = slice(g * HEAD_DIM, (g + 1) * HEAD_DIM)
        o_ref[:, feat] = acc_ref[feat, :].T.astype(o_ref.dtype)


def _sb_attention(q, k, vt, n_heads):
    B, S, D = q.shape
    t = min(SB_TILE, S)
    width = SB_HEADS_PER_STEP * HEAD_DIM
    return pl.pallas_call(
        _sb_kernel,
        grid=(B, n_heads // SB_HEADS_PER_STEP, S // t),
        in_specs=[pl.BlockSpec((None, t, width), lambda b, h, i: (b, i, h)),
                  pl.BlockSpec((None, S, width), lambda b, h, i: (b, 0, h)),
                  pl.BlockSpec((None, width, S), lambda b, h, i: (b, h, 0))],
        out_specs=pl.BlockSpec((None, t, width), lambda b, h, i: (b, i, h)),
        out_shape=jax.ShapeDtypeStruct((B, S, D), BF16),
        scratch_shapes=[pltpu.VMEM((SB_HEADS_PER_STEP, t), F32), pltpu.VMEM((width, t), F32)],
        compiler_params=_params("parallel", "parallel", "arbitrary"),
        name="sb_attention",
    )(q, k, vt)


def _outproj_ln_kernel(o_ref, w_ref, x_ref, gate_ref, g_ref, b_ref, out_ref):
    y = jnp.dot(o_ref[...], w_ref[...], preferred_element_type=F32)
    z = DEEPNORM_ALPHA * x_ref[...] + gate_ref[...] * y
    out_ref[...] = _layer_norm(z, g_ref[...], b_ref[...])


def _out_projection_norm(o, w_bf16, x, gate, ln_g, ln_b):
    B, S, D = x.shape
    tm = min(ROW_TILE, S)
    row_spec = pl.BlockSpec((None, tm, D), lambda b, i: (b, i, 0))
    return pl.pallas_call(
        _outproj_ln_kernel,
        grid=(B, S // tm),
        in_specs=[row_spec,
                  pl.BlockSpec((D, D), lambda b, i: (0, 0)),
                  row_spec,
                  pl.BlockSpec((None, 1, D), lambda b, i: (b, 0, 0)),
                  pl.BlockSpec((1, D), lambda b, i: (0, 0)),
                  pl.BlockSpec((1, D), lambda b, i: (0, 0))],
        out_specs=row_spec,
        out_shape=jax.ShapeDtypeStruct((B, S, D), F32),
        compiler_params=_params("parallel", "parallel"),
        name="out_projection_norm",
    )(o, w_bf16, x, gate.reshape(B, 1, D), ln_g.reshape(1, D), ln_b.reshape(1, D))


def _router_kernel(x_ref, sh_ref, sc_ref, rwt_ref, rb_ref,
                   h_ref, idx_ref, wts_ref, pos_ref, cnt_ref, carry_ref):
    tm = x_ref.shape[0]
    n_experts = rwt_ref.shape[0]

    @pl.when(jnp.logical_and(pl.program_id(0) == 0, pl.program_id(1) == 0))
    def _():
        carry_ref[...] = jnp.zeros_like(carry_ref)

    h = x_ref[...] * (1.0 + sc_ref[...]) + sh_ref[...]
    h_ref[...] = _pack_halves(h)
    logits = lax.dot_general(rwt_ref[...], h, NT_DIMS, preferred_element_type=F32,
                             precision=lax.Precision.HIGHEST)
    scores = jax.nn.sigmoid(logits)
    expert = lax.broadcasted_iota(I32, (n_experts, tm), 0).astype(F32)
    slot = lax.broadcasted_iota(I32, (TOP_K, tm), 0)
    sel = scores + rb_ref[...]

    chosen = jnp.zeros((n_experts, tm), F32)
    idx_acc = jnp.zeros((TOP_K, tm), F32)
    w_acc = jnp.zeros((TOP_K, tm), F32)
    total = jnp.zeros((1, tm), F32)
    picks = []
    for k in range(TOP_K):
        best = jnp.max(sel, axis=0, keepdims=True)
        pick = jnp.min(jnp.where(sel == best, expert, float(n_experts)), axis=0, keepdims=True)
        hit = expert == pick
        score = jnp.sum(jnp.where(hit, scores, 0.0), axis=0, keepdims=True)
        sel = jnp.where(hit, -jnp.inf, sel)
        chosen = jnp.where(hit, 1.0, chosen)
        idx_acc = jnp.where(slot == k, pick, idx_acc)
        w_acc = jnp.where(slot == k, score, w_acc)
        total = total + score
        picks.append(pick)
    w_acc = w_acc / total * ROUTED_SCALE

    row = lax.broadcasted_iota(I32, (tm, tm), 0)
    col = lax.broadcasted_iota(I32, (tm, tm), 1)
    earlier = jnp.where(row < col, 1.0, 0.0).astype(BF16)
    chosen_b = chosen.astype(BF16)
    before = jnp.dot(chosen_b, earlier, preferred_element_type=F32) + carry_ref[...]
    pos_acc = jnp.zeros((TOP_K, tm), F32)
    for k in range(TOP_K):
        rank = jnp.sum(jnp.where(expert == picks[k], before, 0.0), axis=0, keepdims=True)
        pos_acc = jnp.where(slot == k, rank, pos_acc)

    carry_ref[...] += jnp.dot(chosen_b, jnp.ones((tm, tm), BF16), preferred_element_type=F32)
    idx_ref[...] = idx_acc.astype(I32)
    wts_ref[...] = w_acc
    pos_ref[...] = pos_acc.astype(I32)
    cnt_ref[...] = carry_ref[:, :LANES]


def _router(x, shift, scale, router_w, router_bias):
    B, S, D = x.shape
    E = router_w.shape[1]
    tm = min(ROW_TILE, S)
    row_spec = pl.BlockSpec((None, tm, D), lambda b, i: (b, i, 0))
    vec_spec = pl.BlockSpec((None, 1, D), lambda b, i: (b, 0, 0))
    k_spec = pl.BlockSpec((None, TOP_K, tm), lambda b, i: (b, 0, i))
    return pl.pallas_call(
        _router_kernel,
        grid=(B, S // tm),
        in_specs=[row_spec, vec_spec, vec_spec,
                  pl.BlockSpec((E, D), lambda b, i: (0, 0)),
                  pl.BlockSpec((E, tm), lambda b, i: (0, 0))],
        out_specs=[pl.BlockSpec((None, tm, D // 2), lambda b, i: (b, i, 0)), k_spec, k_spec, k_spec,
                   pl.BlockSpec((E, LANES), lambda b, i: (0, 0))],
        out_shape=[jax.ShapeDtypeStruct((B, S, D // 2), U32),
                   jax.ShapeDtypeStruct((B, TOP_K, S), I32),
                   jax.ShapeDtypeStruct((B, TOP_K, S), F32),
                   jax.ShapeDtypeStruct((B, TOP_K, S), I32),
                   jax.ShapeDtypeStruct((E, LANES), F32)],
        scratch_shapes=[pltpu.VMEM((E, tm), F32)],
        compiler_params=_params("arbitrary", "arbitrary"),
        name="router",
    )(x, shift.reshape(B, 1, D), scale.reshape(B, 1, D), router_w.T,
      jnp.broadcast_to(router_bias[:, None], (E, tm)))


def _sc_mesh():
    return plsc.VectorSubcoreMesh(core_axis_name="core", subcore_axis_name="subcore")


def _sc_dispatch(h, dest, n_rows):
    T, D = h.shape
    B, K, S = dest.shape
    W = SC_WINDOW
    per_worker = T // SC_WORKERS
    dest_w = dest.reshape(B, K, S // W, W).transpose(0, 2, 1, 3).reshape(T * K)

    @pl.kernel(out_type=jax.ShapeDtypeStruct((n_rows, D), h.dtype), mesh=_sc_mesh(),
               scratch_types=[pltpu.VMEM((K * W,), I32), pltpu.VMEM((W, D), h.dtype)])
    def dispatch(h_hbm, idx_hbm, rows_hbm, idx_v, buf_v):
        worker = lax.axis_index("core") * SC_SUBCORES + lax.axis_index("subcore")

        @pl.loop(0, per_worker // W)
        def _(s):
            base = pl.multiple_of(worker * per_worker + s * W, W)
            pltpu.sync_copy(idx_hbm.at[pl.ds(base * K, K * W)], idx_v)
            pltpu.sync_copy(h_hbm.at[pl.ds(base, W)], buf_v)
            for k in range(K):
                pltpu.sync_copy(buf_v, rows_hbm.at[idx_v.at[pl.ds(k * W, W)]])

    return dispatch(h, dest_w)


def _sc_gather(rows, index):
    R = index.shape[0]
    D = rows.shape[1]
    W = SC_WINDOW
    per_worker = R // SC_WORKERS

    @pl.kernel(out_type=jax.ShapeDtypeStruct((R, D), rows.dtype), mesh=_sc_mesh(),
               scratch_types=[pltpu.VMEM((W,), I32), pltpu.VMEM((W, D), rows.dtype)])
    def gather(rows_hbm, idx_hbm, out_hbm, idx_v, buf_v):
        worker = lax.axis_index("core") * SC_SUBCORES + lax.axis_index("subcore")

        @pl.loop(0, per_worker // W)
        def _(s):
            base = pl.multiple_of(worker * per_worker + s * W, W)
            pltpu.sync_copy(idx_hbm.at[pl.ds(base, W)], idx_v)
            pltpu.sync_copy(rows_hbm.at[idx_v], buf_v)
            pltpu.sync_copy(buf_v, out_hbm.at[pl.ds(base, W)])

    return gather(rows, index)


def _swiglu_packed(x_packed, w_gate, w_up, w_down):
    half = x_packed.shape[1]
    lo, hi = _unpack_halves(x_packed)
    wg = w_gate.astype(BF16)
    wu = w_up.astype(BF16)
    g = (jnp.dot(lo, wg[:half, :], preferred_element_type=F32)
         + jnp.dot(hi, wg[half:, :], preferred_element_type=F32))
    u = (jnp.dot(lo, wu[:half, :], preferred_element_type=F32)
         + jnp.dot(hi, wu[half:, :], preferred_element_type=F32))
    a = (g * jax.nn.sigmoid(g) * u).astype(BF16)
    return jnp.dot(a, w_down.astype(BF16), preferred_element_type=F32)


def _expert_kernel(be_ref, used_ref, x_ref, wg_ref, wu_ref, wd_ref, o_ref):
    @pl.when(pl.program_id(0) < used_ref[0])
    def _():
        o_ref[...] = _pack_halves(_swiglu_packed(x_ref[...], wg_ref[...], wu_ref[...], wd_ref[...]))


def _experts(rows, block_expert, n_used, layer, w_gate, w_up, w_down):
    R, half = rows.shape
    D = 2 * half
    F = w_gate.shape[3]
    blk = EXPERT_BLOCK
    grid_spec = pltpu.PrefetchScalarGridSpec(
        num_scalar_prefetch=2,
        grid=(R // blk,),
        in_specs=[pl.BlockSpec((blk, half), lambda r, be, nu: (r, 0)),
                  pl.BlockSpec((None, None, D, F), lambda r, be, nu: (layer, be[r], 0, 0)),
                  pl.BlockSpec((None, None, D, F), lambda r, be, nu: (layer, be[r], 0, 0)),
                  pl.BlockSpec((None, None, F, D), lambda r, be, nu: (layer, be[r], 0, 0))],
        out_specs=pl.BlockSpec((blk, half), lambda r, be, nu: (r, 0)),
    )
    return pl.pallas_call(
        _expert_kernel,
        grid_spec=grid_spec,
        out_shape=jax.ShapeDtypeStruct((R, half), U32),
        compiler_params=_params("arbitrary"),
        name="routed_experts",
    )(block_expert, n_used, rows, w_gate, w_up, w_down)


def _moe_out_kernel(ga_ref, w_ref, h_ref, sg_ref, su_ref, sd_ref, x_ref, gate_ref, g_ref, b_ref, out_ref):
    half = h_ref.shape[1]
    y = _swiglu_packed(h_ref[...], sg_ref[...], su_ref[...], sd_ref[...])
    w = w_ref[...]
    y_lo = y[:, :half]
    y_hi = y[:, half:]
    for k in range(TOP_K):
        packed = ga_ref[k]
        wk = w[:, k:k + 1]
        y_lo = y_lo + wk * lax.bitcast_convert_type(packed << 16, F32)
        y_hi = y_hi + wk * lax.bitcast_convert_type(packed & jnp.uint32(0xFFFF0000), F32)
    y = jnp.concatenate([y_lo, y_hi], axis=1)
    z = DEEPNORM_ALPHA * x_ref[...] + gate_ref[...] * y
    out_ref[...] = _layer_norm(z, g_ref[...], b_ref[...])


def _moe_output_norm(gathered, wts, h, layer, s_gate, s_up, s_down, x, gate, ln_g, ln_b):
    B, S, D = x.shape
    half = D // 2
    F = s_gate.shape[2]
    tm = min(ROW_TILE // 2, S)
    row_spec = pl.BlockSpec((None, tm, D), lambda b, i: (b, i, 0))
    return pl.pallas_call(
        _moe_out_kernel,
        grid=(B, S // tm),
        in_specs=[pl.BlockSpec((TOP_K, None, tm, half), lambda b, i: (0, b, i, 0)),
                  pl.BlockSpec((None, tm, TOP_K), lambda b, i: (b, i, 0)),
                  pl.BlockSpec((None, tm, half), lambda b, i: (b, i, 0)),
                  pl.BlockSpec((None, D, F), lambda b, i: (layer, 0, 0)),
                  pl.BlockSpec((None, D, F), lambda b, i: (layer, 0, 0)),
                  pl.BlockSpec((None, F, D), lambda b, i: (layer, 0, 0)),
                  row_spec,
                  pl.BlockSpec((None, 1, D), lambda b, i: (b, 0, 0)),
                  pl.BlockSpec((1, D), lambda b, i: (0, 0)),
                  pl.BlockSpec((1, D), lambda b, i: (0, 0))],
        out_specs=row_spec,
        out_shape=jax.ShapeDtypeStruct((B, S, D), F32),
        compiler_params=_params("parallel", "parallel"),
        name="moe_output_norm",
    )(gathered.reshape(TOP_K, B, S, half), wts, h, s_gate, s_up, s_down, x,
      gate.reshape(B, 1, D), ln_g.reshape(1, D), ln_b.reshape(1, D))


def _moe_layer(x, shift, scale, gate, router_w, router_bias, layer, w_gate, w_up, w_down,
               s_gate, s_up, s_down, ln_g, ln_b):
    B, S, D = x.shape
    T = B * S
    E = router_w.shape[1]
    blk = EXPERT_BLOCK
    h, idx, wts, pos, cnt = _router(x, shift, scale, router_w, router_bias)

    counts = cnt[:, 0].astype(I32)
    padded = (counts + blk - 1) // blk * blk
    pend = jnp.cumsum(padded)
    pstart = pend - padded
    n_blocks = -(-T * TOP_K // blk) + E
    onehot = idx[..., None] == jnp.arange(E, dtype=I32)
    dest = jnp.sum(jnp.where(onehot, pstart, 0), axis=-1) + pos
    block_expert = jnp.minimum(
        jnp.sum((jnp.arange(n_blocks, dtype=I32)[:, None] * blk >= pend[None, :]).astype(I32), axis=1),
        E - 1).astype(I32)
    n_used = (pend[-1:] // blk).astype(I32)

    rows = _sc_dispatch(h.reshape(T, D // 2), dest, n_blocks * blk)
    out_rows = _experts(rows, block_expert, n_used, layer, w_gate, w_up, w_down)
    gathered = _sc_gather(out_rows, dest.transpose(1, 0, 2).reshape(TOP_K * T))
    return _moe_output_norm(gathered, wts.transpose(0, 2, 1), h, layer, s_gate, s_up, s_down, x, gate, ln_g, ln_b)


def kernel(x, c, fox_w_in, fox_b_f, fox_w_o, sb_w_in, sb_w_o, ada_w, ada_b, ln_attn_g, ln_attn_b,
           router_w, router_bias, exp_w_gate, exp_w_up, exp_w_down,
           shared_w_gate, shared_w_up, shared_w_down, ln_ffn_g, ln_ffn_b):
    B, S, D = x.shape
    H = D // HEAD_DIM
    depth = ada_w.shape[0]
    mod = _ada_mod(c, ada_w, ada_b)
    for i in range(depth):
        sh_a, sc_a, g_a, sh_m, sc_m, g_m = jnp.split(mod[i], 6, axis=-1)
        j = i // 2
        w_in = fox_w_in[j] if i % 2 == 0 else sb_w_in[j]
        wq = (w_in[:, :D] * ATTN_SCALE).astype(BF16)
        wk = w_in[:, D:2 * D].astype(BF16)
        wvt = w_in[:, 2 * D:3 * D].T.astype(BF16)
        if i % 2 == 0:
            wf = jnp.zeros((D, LANES), F32).at[:, :H].set(w_in[:, 3 * D:]).astype(BF16)
            bf = jnp.zeros((1, LANES), F32).at[0, :H].set(fox_b_f[j])
            q, k_aug, vt, fcum, stats = _in_projection_forget(x, sh_a, sc_a, wq, wk, wvt, wf, bf)
            o = _fox_attention(q, k_aug, vt, fcum, stats, H)
            w_o = fox_w_o[j]
        else:
            q, k, vt = _in_projection(x, sh_a, sc_a, wq, wk, wvt)
            o = _sb_attention(q, k, vt, H)
            w_o = sb_w_o[j]
        x = _out_projection_norm(o, w_o.astype(BF16), x, 1.0 + g_a, ln_attn_g[i], ln_attn_b[i])
        x = _moe_layer(x, sh_m, sc_m, 1.0 + g_m, router_w[i], router_bias[i], i,
                       exp_w_gate, exp_w_up, exp_w_down,
                       shared_w_gate, shared_w_up, shared_w_down, ln_ffn_g[i], ln_ffn_b[i])
    return x
```

```python
import functools
import math

import jax
import jax.numpy as jnp
from jax import lax
from jax.experimental import pallas as pl
from jax.experimental.pallas import tpu as pltpu
from jax.experimental.pallas import tpu_sc as plsc

F32 = jnp.float32
BF16 = jnp.bfloat16
I32 = jnp.int32
U32 = jnp.uint32

HEAD_DIM = 128
TOP_K = 8
ROUTED_SCALE = 2.5
LN_EPS = 1e-5
DEPTH = 2
DEEPNORM_ALPHA = (2 * DEPTH) ** 0.25
ATTN_SCALE = 1.0 / math.sqrt(HEAD_DIM)

LANES = 128
UNDERFLOW_LOG = 105.0
VMEM_LIMIT = 56 * 1024 * 1024

ROW_TILE = 512
ATTN_TILE = 512
SB_TILE = 256
SB_HEADS_PER_STEP = 2
EXPERT_BLOCK = 512
SC_WINDOW = 64
SC_SUBCORES = 16
SC_WORKERS = 2 * SC_SUBCORES
N_BIAS_COLS = 3
BOUND_SLACK = 1.008

NT_DIMS = (((1,), (1,)), ((), ()))


def _params(*sem):
    return pltpu.CompilerParams(dimension_semantics=sem, vmem_limit_bytes=VMEM_LIMIT)


def _log_sigmoid(z):
    return jnp.minimum(z, 0.0) - jnp.log1p(jnp.exp(-jnp.abs(z)))


def _split3(x):
    hi = x.astype(BF16)
    r = x - hi.astype(F32)
    mid = r.astype(BF16)
    lo = (r - mid.astype(F32)).astype(BF16)
    return hi, mid, lo


def _pack_halves(x):
    n = x.shape[1] // 2
    bits = lax.bitcast_convert_type(x.astype(BF16).astype(F32), U32)
    return (bits[:, :n] >> 16) | (bits[:, n:] & jnp.uint32(0xFFFF0000))


def _unpack_halves(w):
    lo = lax.bitcast_convert_type(w << 16, F32).astype(BF16)
    hi = lax.bitcast_convert_type(w & jnp.uint32(0xFFFF0000), F32).astype(BF16)
    return lo, hi


def _layer_norm(z, g, b):
    mu = jnp.mean(z, axis=-1, keepdims=True)
    d = z - mu
    var = jnp.mean(d * d, axis=-1, keepdims=True)
    return d * lax.rsqrt(var + LN_EPS) * g + b


def _ada_kernel(c_ref, w_ref, b_ref, o_ref):
    c = c_ref[...]
    act = c * jax.nn.sigmoid(c)
    o_ref[...] = jnp.dot(act, w_ref[...], preferred_element_type=F32,
                         precision=lax.Precision.HIGHEST) + b_ref[...]


def _ada_mod(c, ada_w, ada_b):
    B, D = c.shape
    L, _, N = ada_w.shape
    cp = jnp.zeros((8, D), F32).at[:B].set(c)
    out = pl.pallas_call(
        _ada_kernel,
        grid=(L, N // D),
        in_specs=[pl.BlockSpec((8, D), lambda l, j: (0, 0)),
                  pl.BlockSpec((None, D, D), lambda l, j: (l, 0, j)),
                  pl.BlockSpec((None, 1, D), lambda l, j: (l, 0, j))],
        out_specs=pl.BlockSpec((None, 8, D), lambda l, j: (l, 0, j)),
        out_shape=jax.ShapeDtypeStruct((L, 8, N), F32),
        compiler_params=_params("parallel", "parallel"),
        name="ada_mod",
    )(cp, ada_w, ada_b.reshape(L, 1, N))
    return out[:, :B]


def _project_qkv(h, wq_ref, wk_ref, wvt_ref, q_ref, k_ref, vt_ref):
    q_ref[...] = jnp.dot(h, wq_ref[...], preferred_element_type=F32).astype(BF16)
    k_ref[...] = jnp.dot(h, wk_ref[...], preferred_element_type=F32).astype(BF16)
    vt_ref[...] = lax.dot_general(wvt_ref[...], h, NT_DIMS, preferred_element_type=F32).astype(BF16)


def _inproj_kernel(x_ref, sh_ref, sc_ref, wq_ref, wk_ref, wvt_ref, q_ref, k_ref, vt_ref):
    h = (x_ref[...] * (1.0 + sc_ref[...]) + sh_ref[...]).astype(BF16)
    _project_qkv(h, wq_ref, wk_ref, wvt_ref, q_ref, k_ref, vt_ref)


def _inproj_forget_kernel(x_ref, sh_ref, sc_ref, wq_ref, wk_ref, wvt_ref, wf_ref, bf_ref, sel_ref, hsum_ref,
                          q_ref, ka_ref, vt_ref, f_ref, stats_ref, carry_ref):
    tm, D = x_ref.shape
    n_heads = D // HEAD_DIM
    h = (x_ref[...] * (1.0 + sc_ref[...]) + sh_ref[...]).astype(BF16)
    q = jnp.dot(h, wq_ref[...], preferred_element_type=F32).astype(BF16)
    k = jnp.dot(h, wk_ref[...], preferred_element_type=F32).astype(BF16)
    q_ref[...] = q
    vt_ref[...] = lax.dot_general(wvt_ref[...], h, NT_DIMS, preferred_element_type=F32).astype(BF16)

    @pl.when(pl.program_id(1) == 0)
    def _():
        carry_ref[...] = jnp.zeros_like(carry_ref)

    logf = _log_sigmoid(jnp.dot(h, wf_ref[...], preferred_element_type=F32) + bf_ref[...])
    row = lax.broadcasted_iota(I32, (tm, tm), 0)
    col = lax.broadcasted_iota(I32, (tm, tm), 1)
    tri = jnp.where(row >= col, 1.0, 0.0).astype(BF16)
    hi, mid, lo = _split3(logf)
    cs = (jnp.dot(tri, hi, preferred_element_type=F32)
          + jnp.dot(tri, mid, preferred_element_type=F32)
          + jnp.dot(tri, lo, preferred_element_type=F32))
    cum = cs + carry_ref[...]
    f_ref[...] = cum
    carry_ref[...] = cum[tm - 1:tm, :]

    pieces = jnp.concatenate(_split3(-cum), axis=1)
    for g in range(n_heads):
        ka_ref[:, 2 * g * LANES:(2 * g + 1) * LANES] = k[:, g * HEAD_DIM:(g + 1) * HEAD_DIM]
        ka_ref[:, (2 * g + 1) * LANES:(2 * g + 2) * LANES] = jnp.dot(
            pieces, sel_ref[g], preferred_element_type=F32).astype(BF16)

    qf = q.astype(F32)
    kf = k.astype(F32)
    stats_ref[...] = (jnp.dot((qf * qf).astype(BF16), hsum_ref[0], preferred_element_type=F32)
                      + jnp.dot((kf * kf).astype(BF16), hsum_ref[1], preferred_element_type=F32)
                      + jnp.dot((qf * kf).astype(BF16), hsum_ref[2], preferred_element_type=F32))


def _in_projection(x, shift, scale, wq, wk, wvt):
    B, S, D = x.shape
    tm = min(ROW_TILE, S)
    row_spec = pl.BlockSpec((None, tm, D), lambda b, i: (b, i, 0))
    vec_spec = pl.BlockSpec((None, 1, D), lambda b, i: (b, 0, 0))
    w_spec = pl.BlockSpec((D, D), lambda b, i: (0, 0))
    return pl.pallas_call(
        _inproj_kernel,
        grid=(B, S // tm),
        in_specs=[row_spec, vec_spec, vec_spec, w_spec, w_spec, w_spec],
        out_specs=[row_spec, row_spec, pl.BlockSpec((None, D, tm), lambda b, i: (b, 0, i))],
        out_shape=[jax.ShapeDtypeStruct((B, S, D), BF16), jax.ShapeDtypeStruct((B, S, D), BF16),
                   jax.ShapeDtypeStruct((B, D, S), BF16)],
        compiler_params=_params("parallel", "parallel"),
        name="in_projection",
    )(x, shift.reshape(B, 1, D), scale.reshape(B, 1, D), wq, wk, wvt)


def _in_projection_forget(x, shift, scale, wq, wk, wvt, wf, bf):
    B, S, D = x.shape
    n_heads = D // HEAD_DIM
    tm = min(ROW_TILE, S)
    row_spec = pl.BlockSpec((None, tm, D), lambda b, i: (b, i, 0))
    vec_spec = pl.BlockSpec((None, 1, D), lambda b, i: (b, 0, 0))
    w_spec = pl.BlockSpec((D, D), lambda b, i: (0, 0))
    lane_spec = pl.BlockSpec((None, tm, LANES), lambda b, i: (b, i, 0))
    heads = jnp.arange(n_heads)
    sel = jnp.zeros((n_heads, N_BIAS_COLS * LANES, LANES), BF16)
    for p in range(N_BIAS_COLS):
        sel = sel.at[heads, p * LANES + heads, p].set(1.0)
    feat = jnp.arange(D)
    hsum = jnp.zeros((3, D, LANES), BF16)
    for c in range(3):
        hsum = hsum.at[c, feat, c * n_heads + feat // HEAD_DIM].set(1.0)
    return pl.pallas_call(
        _inproj_forget_kernel,
        grid=(B, S // tm),
        in_specs=[row_spec, vec_spec, vec_spec, w_spec, w_spec, w_spec,
                  pl.BlockSpec((D, LANES), lambda b, i: (0, 0)),
                  pl.BlockSpec((1, LANES), lambda b, i: (0, 0)),
                  pl.BlockSpec((n_heads, N_BIAS_COLS * LANES, LANES), lambda b, i: (0, 0, 0)),
                  pl.BlockSpec((3, D, LANES), lambda b, i: (0, 0, 0))],
        out_specs=[row_spec, pl.BlockSpec((None, tm, 2 * D), lambda b, i: (b, i, 0)),
                   pl.BlockSpec((None, D, tm), lambda b, i: (b, 0, i)), lane_spec, lane_spec],
        out_shape=[jax.ShapeDtypeStruct((B, S, D), BF16), jax.ShapeDtypeStruct((B, S, 2 * D), BF16),
                   jax.ShapeDtypeStruct((B, D, S), BF16), jax.ShapeDtypeStruct((B, S, LANES), F32),
                   jax.ShapeDtypeStruct((B, S, LANES), F32)],
        scratch_shapes=[pltpu.VMEM((1, LANES), F32)],
        compiler_params=_params("parallel", "arbitrary"),
        name="in_projection_forget",
    )(x, shift.reshape(B, 1, D), scale.reshape(B, 1, D), wq, wk, wvt, wf, bf, sel, hsum)


def _fox_kernel(js_ref, q_ref, ka_ref, vt_ref, o_ref, m_ref, l_ref, acc_ref, s_ref, *, n_heads):
    t = q_ref.shape[0]
    b, h, i = pl.program_id(0), pl.program_id(1), pl.program_id(2)
    nq = pl.num_programs(2)
    m_ref[...] = jnp.full(m_ref.shape, -jnp.inf, F32)
    l_ref[...] = jnp.zeros_like(l_ref)
    acc_ref[...] = jnp.zeros_like(acc_ref)
    sub = lax.broadcasted_iota(I32, (LANES, t), 0)
    q_aug_t = jnp.concatenate([q_ref[...].astype(F32).T.astype(BF16),
                               jnp.where(sub < N_BIAS_COLS, 1.0, 0.0).astype(BF16)], axis=0)

    def scores(j, slot):
        ks = pl.multiple_of(j * t, t)
        s_ref[slot] = jnp.dot(ka_ref[pl.ds(ks, t), :], q_aug_t, preferred_element_type=F32)

    def accumulate(j, slot, masked):
        ks = pl.multiple_of(j * t, t)
        s = s_ref[slot]
        if masked:
            key = lax.broadcasted_iota(I32, (t, t), 0)
            qry = lax.broadcasted_iota(I32, (t, t), 1)
            s = jnp.where(key <= qry, s, -jnp.inf)
        m_prev = m_ref[...]
        m_new = jnp.maximum(m_prev, jnp.max(s, axis=0, keepdims=True))
        alpha = jnp.exp(m_prev - m_new)
        p = jnp.exp(s - m_new)
        l_ref[...] = alpha * l_ref[...] + jnp.sum(p, axis=0, keepdims=True)
        acc_ref[...] = alpha * acc_ref[...] + jnp.dot(
            vt_ref[:, pl.ds(ks, t)], p.astype(BF16), preferred_element_type=F32)
        m_ref[...] = m_new

    first = js_ref[(b * n_heads + h) * nq + i]
    n_full = i - first
    scores(first, 0)

    def body(pair, carry):
        j = first + 2 * pair
        scores(j + 1, 1)
        accumulate(j, 0, False)
        scores(j + 2, 0)
        accumulate(j + 1, 1, False)
        return carry

    lax.fori_loop(0, n_full // 2, body, 0)

    @pl.when(n_full % 2 == 1)
    def _():
        scores(i, 1)
        accumulate(i - 1, 0, False)
        accumulate(i, 1, True)

    @pl.when(n_full % 2 == 0)
    def _():
        accumulate(i, 0, True)

    o_ref[...] = (acc_ref[...] / l_ref[...]).T.astype(o_ref.dtype)


def _fox_first_blocks(stats, fcum, n_heads, t):
    B, S, _ = stats.shape
    qn = jnp.sqrt(stats[..., :n_heads])
    kn = jnp.sqrt(jnp.max(stats[..., n_heads:2 * n_heads], axis=1))
    diag = stats[..., 2 * n_heads:3 * n_heads]
    f = fcum[..., :n_heads]
    need = qn * kn[:, None, :] * BOUND_SLACK - diag + f + (UNDERFLOW_LOG + 1.0)
    nq = S // t
    need = jnp.max(need.reshape(B, nq, t, n_heads), axis=2)
    f_end = f.reshape(B, nq, t, n_heads)[:, :, t - 1, :]
    skippable = f_end[:, None, :, :] > need[:, :, None, :]
    first = jnp.sum(skippable.astype(I32), axis=2)
    first = jnp.minimum(first, jnp.arange(nq, dtype=I32)[None, :, None])
    return first.transpose(0, 2, 1).reshape(-1)


def _fox_attention(q, k_aug, vt, fcum, stats, n_heads):
    B, S, D = q.shape
    t = min(ATTN_TILE, S)
    nq = S // t
    first = _fox_first_blocks(stats, fcum, n_heads, t)
    grid_spec = pltpu.PrefetchScalarGridSpec(
        num_scalar_prefetch=1,
        grid=(B, n_heads, nq),
        in_specs=[pl.BlockSpec((None, t, HEAD_DIM), lambda b, h, i, js: (b, i, h)),
                  pl.BlockSpec((None, S, 2 * LANES), lambda b, h, i, js: (b, 0, h)),
                  pl.BlockSpec((None, HEAD_DIM, S), lambda b, h, i, js: (b, h, 0))],
        out_specs=pl.BlockSpec((None, t, HEAD_DIM), lambda b, h, i, js: (b, i, h)),
        scratch_shapes=[pltpu.VMEM((1, t), F32), pltpu.VMEM((1, t), F32),
                        pltpu.VMEM((HEAD_DIM, t), F32), pltpu.VMEM((2, t, t), F32)],
    )
    return pl.pallas_call(
        functools.partial(_fox_kernel, n_heads=n_heads),
        grid_spec=grid_spec,
        out_shape=jax.ShapeDtypeStruct((B, S, D), BF16),
        compiler_params=_params("parallel", "parallel", "arbitrary"),
        name="fox_attention",
    )(first, q, k_aug, vt)


def _sb_kernel(q_ref, k_ref, vt_ref, o_ref, carry_ref, acc_ref):
    t = q_ref.shape[0]
    n_local = q_ref.shape[1] // HEAD_DIM
    i = pl.program_id(2)
    carry_ref[...] = jnp.zeros_like(carry_ref)
    acc_ref[...] = jnp.zeros_like(acc_ref)
    key = lax.broadcasted_iota(I32, (t, t), 0)
    qry = lax.broadcasted_iota(I32, (t, t), 1)
    later = jnp.where(qry > key, 1.0, 0.0).astype(BF16)
    later2 = jnp.concatenate([later, later], axis=1)

    def block(j, masked):
        ks = pl.multiple_of(j * t, t)
        feats = [slice(g * HEAD_DIM, (g + 1) * HEAD_DIM) for g in range(n_local)]
        zs = [lax.dot_general(k_ref[pl.ds(ks, t), f], q_ref[:, f], NT_DIMS, preferred_element_type=F32)
              for f in feats]
        log1ms, suffixes = [], []
        for z in zs:
            log1m = _log_sigmoid(-z)
            if masked:
                log1m = jnp.where(key < qry, log1m, 0.0)
            hi = log1m.astype(BF16)
            mid = (log1m - hi.astype(F32)).astype(BF16)
            log1ms.append(log1m)
            suffixes.append(jnp.dot(later2, jnp.concatenate([hi, mid], axis=0), preferred_element_type=F32))
        for g, f in enumerate(feats):
            la = zs[g] + log1ms[g] + suffixes[g] + carry_ref[g:g + 1, :]
            if masked:
                la = jnp.where(key < qry, la, -jnp.inf)
            a = jnp.exp(la)
            acc_ref[f, :] += jnp.dot(vt_ref[f, pl.ds(ks, t)], a.astype(BF16), preferred_element_type=F32)
            carry_ref[g:g + 1, :] += jnp.sum(log1ms[g], axis=0, keepdims=True)

    block(i, True)

    def cond(state):
        j, top = state
        return jnp.logical_and(j >= 0, top > -UNDERFLOW_LOG)

    def body(state):
        j, _ = state
        block(j, False)
        return j - 1, jnp.max(carry_ref[...])

    lax.while_loop(cond, body, (i - 1, jnp.max(carry_ref[...])))
    for g in range(n_local):
        feat = slice(g * HEAD_DIM, (g + 1) * HEAD_DIM)
        o_ref[:, feat] = acc_ref[feat, :].T.astype(o_ref.dtype)


def _sb_attention(q, k, vt, n_heads):
    B, S, D = q.shape
    t = min(SB_TILE, S)
    width = SB_HEADS_PER_STEP * HEAD_DIM
    return pl.pallas_call(
        _sb_kernel,
        grid=(B, n_heads // SB_HEADS_PER_STEP, S // t),
        in_specs=[pl.BlockSpec((None, t, width), lambda b, h, i: (b, i, h)),
                  pl.BlockSpec((None, S, width), lambda b, h, i: (b, 0, h)),
                  pl.BlockSpec((None, width, S), lambda b, h, i: (b, h, 0))],
        out_specs=pl.BlockSpec((None, t, width), lambda b, h, i: (b, i, h)),
        out_shape=jax.ShapeDtypeStruct((B, S, D), BF16),
        scratch_shapes=[pltpu.VMEM((SB_HEADS_PER_STEP, t), F32), pltpu.VMEM((width, t), F32)],
        compiler_params=_params("parallel", "parallel", "arbitrary"),
        name="sb_attention",
    )(q, k, vt)


def _outproj_ln_kernel(o_ref, w_ref, x_ref, gate_ref, g_ref, b_ref, out_ref):
    y = jnp.dot(o_ref[...], w_ref[...], preferred_element_type=F32)
    z = DEEPNORM_ALPHA * x_ref[...] + gate_ref[...] * y
    out_ref[...] = _layer_norm(z, g_ref[...], b_ref[...])


def _out_projection_norm(o, w_bf16, x, gate, ln_g, ln_b):
    B, S, D = x.shape
    tm = min(ROW_TILE, S)
    row_spec = pl.BlockSpec((None, tm, D), lambda b, i: (b, i, 0))
    return pl.pallas_call(
        _outproj_ln_kernel,
        grid=(B, S // tm),
        in_specs=[row_spec,
                  pl.BlockSpec((D, D), lambda b, i: (0, 0)),
                  row_spec,
                  pl.BlockSpec((None, 1, D), lambda b, i: (b, 0, 0)),
                  pl.BlockSpec((1, D), lambda b, i: (0, 0)),
                  pl.BlockSpec((1, D), lambda b, i: (0, 0))],
        out_specs=row_spec,
        out_shape=jax.ShapeDtypeStruct((B, S, D), F32),
        compiler_params=_params("parallel", "parallel"),
        name="out_projection_norm",
    )(o, w_bf16, x, gate.reshape(B, 1, D), ln_g.reshape(1, D), ln_b.reshape(1, D))


def _router_kernel(x_ref, sh_ref, sc_ref, rwt_ref, rb_ref,
                   h_ref, idx_ref, wts_ref, pos_ref, cnt_ref, carry_ref):
    tm = x_ref.shape[0]
    n_experts = rwt_ref.shape[0]

    @pl.when(jnp.logical_and(pl.program_id(0) == 0, pl.program_id(1) == 0))
    def _():
        carry_ref[...] = jnp.zeros_like(carry_ref)

    h = x_ref[...] * (1.0 + sc_ref[...]) + sh_ref[...]
    h_ref[...] = _pack_halves(h)
    logits = lax.dot_general(rwt_ref[...], h, NT_DIMS, preferred_element_type=F32,
                             precision=lax.Precision.HIGHEST)
    scores = jax.nn.sigmoid(logits)
    expert = lax.broadcasted_iota(I32, (n_experts, tm), 0).astype(F32)
    slot = lax.broadcasted_iota(I32, (TOP_K, tm), 0)
    sel = scores + rb_ref[...]

    chosen = jnp.zeros((n_experts, tm), F32)
    idx_acc = jnp.zeros((TOP_K, tm), F32)
    w_acc = jnp.zeros((TOP_K, tm), F32)
    total = jnp.zeros((1, tm), F32)
    picks = []
    for k in range(TOP_K):
        best = jnp.max(sel, axis=0, keepdims=True)
        pick = jnp.min(jnp.where(sel == best, expert, float(n_experts)), axis=0, keepdims=True)
        hit = expert == pick
        score = jnp.sum(jnp.where(hit, scores, 0.0), axis=0, keepdims=True)
        sel = jnp.where(hit, -jnp.inf, sel)
        chosen = jnp.where(hit, 1.0, chosen)
        idx_acc = jnp.where(slot == k, pick, idx_acc)
        w_acc = jnp.where(slot == k, score, w_acc)
        total = total + score
        picks.append(pick)
    w_acc = w_acc / total * ROUTED_SCALE

    row = lax.broadcasted_iota(I32, (tm, tm), 0)
    col = lax.broadcasted_iota(I32, (tm, tm), 1)
    earlier = jnp.where(row < col, 1.0, 0.0).astype(BF16)
    chosen_b = chosen.astype(BF16)
    before = jnp.dot(chosen_b, earlier, preferred_element_type=F32) + carry_ref[...]
    pos_acc = jnp.zeros((TOP_K, tm), F32)
    for k in range(TOP_K):
        rank = jnp.sum(jnp.where(expert == picks[k], before, 0.0), axis=0, keepdims=True)
        pos_acc = jnp.where(slot == k, rank, pos_acc)

    carry_ref[...] += jnp.dot(chosen_b, jnp.ones((tm, tm), BF16), preferred_element_type=F32)
    idx_ref[...] = idx_acc.astype(I32)
    wts_ref[...] = w_acc
    pos_ref[...] = pos_acc.astype(I32)
    cnt_ref[...] = carry_ref[:, :LANES]


def _router(x, shift, scale, router_w, router_bias):
    B, S, D = x.shape
    E = router_w.shape[1]
    tm = min(ROW_TILE, S)
    row_spec = pl.BlockSpec((None, tm, D), lambda b, i: (b, i, 0))
    vec_spec = pl.BlockSpec((None, 1, D), lambda b, i: (b, 0, 0))
    k_spec = pl.BlockSpec((None, TOP_K, tm), lambda b, i: (b, 0, i))
    return pl.pallas_call(
        _router_kernel,
        grid=(B, S // tm),
        in_specs=[row_spec, vec_spec, vec_spec,
                  pl.BlockSpec((E, D), lambda b, i: (0, 0)),
                  pl.BlockSpec((E, tm), lambda b, i: (0, 0))],
        out_specs=[pl.BlockSpec((None, tm, D // 2), lambda b, i: (b, i, 0)), k_spec, k_spec, k_spec,
                   pl.BlockSpec((E, LANES), lambda b, i: (0, 0))],
        out_shape=[jax.ShapeDtypeStruct((B, S, D // 2), U32),
                   jax.ShapeDtypeStruct((B, TOP_K, S), I32),
                   jax.ShapeDtypeStruct((B, TOP_K, S), F32),
                   jax.ShapeDtypeStruct((B, TOP_K, S), I32),
                   jax.ShapeDtypeStruct((E, LANES), F32)],
        scratch_shapes=[pltpu.VMEM((E, tm), F32)],
        compiler_params=_params("arbitrary", "arbitrary"),
        name="router",
    )(x, shift.reshape(B, 1, D), scale.reshape(B, 1, D), router_w.T,
      jnp.broadcast_to(router_bias[:, None], (E, tm)))


def _sc_mesh():
    return plsc.VectorSubcoreMesh(core_axis_name="core", subcore_axis_name="subcore")


def _sc_dispatch(h, dest, n_rows):
    T, D = h.shape
    B, K, S = dest.shape
    W = SC_WINDOW
    per_worker = T // SC_WORKERS
    dest_w = dest.reshape(B, K, S // W, W).transpose(0, 2, 1, 3).reshape(T * K)

    @pl.kernel(out_type=jax.ShapeDtypeStruct((n_rows, D), h.dtype), mesh=_sc_mesh(),
               scratch_types=[pltpu.VMEM((K * W,), I32), pltpu.VMEM((W, D), h.dtype)])
    def dispatch(h_hbm, idx_hbm, rows_hbm, idx_v, buf_v):
        worker = lax.axis_index("core") * SC_SUBCORES + lax.axis_index("subcore")

        @pl.loop(0, per_worker // W)
        def _(s):
            base = pl.multiple_of(worker * per_worker + s * W, W)
            pltpu.sync_copy(idx_hbm.at[pl.ds(base * K, K * W)], idx_v)
            pltpu.sync_copy(h_hbm.at[pl.ds(base, W)], buf_v)
            for k in range(K):
                pltpu.sync_copy(buf_v, rows_hbm.at[idx_v.at[pl.ds(k * W, W)]])

    return dispatch(h, dest_w)


def _sc_gather(rows, index):
    R = index.shape[0]
    D = rows.shape[1]
    W = SC_WINDOW
    per_worker = R // SC_WORKERS

    @pl.kernel(out_type=jax.ShapeDtypeStruct((R, D), rows.dtype), mesh=_sc_mesh(),
               scratch_types=[pltpu.VMEM((W,), I32), pltpu.VMEM((W, D), rows.dtype)])
    def gather(rows_hbm, idx_hbm, out_hbm, idx_v, buf_v):
        worker = lax.axis_index("core") * SC_SUBCORES + lax.axis_index("subcore")

        @pl.loop(0, per_worker // W)
        def _(s):
            base = pl.multiple_of(worker * per_worker + s * W, W)
            pltpu.sync_copy(idx_hbm.at[pl.ds(base, W)], idx_v)
            pltpu.sync_copy(rows_hbm.at[idx_v], buf_v)
            pltpu.sync_copy(buf_v, out_hbm.at[pl.ds(base, W)])

    return gather(rows, index)


def _swiglu_packed(x_packed, w_gate, w_up, w_down):
    half = x_packed.shape[1]
    lo, hi = _unpack_halves(x_packed)
    wg = w_gate.astype(BF16)
    wu = w_up.astype(BF16)
    g = (jnp.dot(lo, wg[:half, :], preferred_element_type=F32)
         + jnp.dot(hi, wg[half:, :], preferred_element_type=F32))
    u = (jnp.dot(lo, wu[:half, :], preferred_element_type=F32)
         + jnp.dot(hi, wu[half:, :], preferred_element_type=F32))
    a = (g * jax.nn.sigmoid(g) * u).astype(BF16)
    return jnp.dot(a, w_down.astype(BF16), preferred_element_type=F32)


def _expert_kernel(be_ref, used_ref, x_ref, wg_ref, wu_ref, wd_ref, o_ref):
    @pl.when(pl.program_id(0) < used_ref[0])
    def _():
        o_ref[...] = _pack_halves(_swiglu_packed(x_ref[...], wg_ref[...], wu_ref[...], wd_ref[...]))


def _experts(rows, block_expert, n_used, layer, w_gate, w_up, w_down):
    R, half = rows.shape
    D = 2 * half
    F = w_gate.shape[3]
    blk = EXPERT_BLOCK
    grid_spec = pltpu.PrefetchScalarGridSpec(
        num_scalar_prefetch=2,
        grid=(R // blk,),
        in_specs=[pl.BlockSpec((blk, half), lambda r, be, nu: (r, 0)),
                  pl.BlockSpec((None, None, D, F), lambda r, be, nu: (layer, be[r], 0, 0)),
                  pl.BlockSpec((None, None, D, F), lambda r, be, nu: (layer, be[r], 0, 0)),
                  pl.BlockSpec((None, None, F, D), lambda r, be, nu: (layer, be[r], 0, 0))],
        out_specs=pl.BlockSpec((blk, half), lambda r, be, nu: (r, 0)),
    )
    return pl.pallas_call(
        _expert_kernel,
        grid_spec=grid_spec,
        out_shape=jax.ShapeDtypeStruct((R, half), U32),
        compiler_params=_params("arbitrary"),
        name="routed_experts",
    )(block_expert, n_used, rows, w_gate, w_up, w_down)


def _moe_out_kernel(ga_ref, w_ref, h_ref, sg_ref, su_ref, sd_ref, x_ref, gate_ref, g_ref, b_ref, out_ref):
    half = h_ref.shape[1]
    y = _swiglu_packed(h_ref[...], sg_ref[...], su_ref[...], sd_ref[...])
    w = w_ref[...]
    y_lo = y[:, :half]
    y_hi = y[:, half:]
    for k in range(TOP_K):
        packed = ga_ref[k]
        wk = w[:, k:k + 1]
        y_lo = y_lo + wk * lax.bitcast_convert_type(packed << 16, F32)
        y_hi = y_hi + wk * lax.bitcast_convert_type(packed & jnp.uint32(0xFFFF0000), F32)
    y = jnp.concatenate([y_lo, y_hi], axis=1)
    z = DEEPNORM_ALPHA * x_ref[...] + gate_ref[...] * y
    out_ref[...] = _layer_norm(z, g_ref[...], b_ref[...])


def _moe_output_norm(gathered, wts, h, layer, s_gate, s_up, s_down, x, gate, ln_g, ln_b):
    B, S, D = x.shape
    half = D // 2
    F = s_gate.shape[2]
    tm = min(ROW_TILE // 2, S)
    row_spec = pl.BlockSpec((None, tm, D), lambda b, i: (b, i, 0))
    return pl.pallas_call(
        _moe_out_kernel,
        grid=(B, S // tm),
        in_specs=[pl.BlockSpec((TOP_K, None, tm, half), lambda b, i: (0, b, i, 0)),
                  pl.BlockSpec((None, tm, TOP_K), lambda b, i: (b, i, 0)),
                  pl.BlockSpec((None, tm, half), lambda b, i: (b, i, 0)),
                  pl.BlockSpec((None, D, F), lambda b, i: (layer, 0, 0)),
                  pl.BlockSpec((None, D, F), lambda b, i: (layer, 0, 0)),
                  pl.BlockSpec((None, F, D), lambda b, i: (layer, 0, 0)),
                  row_spec,
                  pl.BlockSpec((None, 1, D), lambda b, i: (b, 0, 0)),
                  pl.BlockSpec((1, D), lambda b, i: (0, 0)),
                  pl.BlockSpec((1, D), lambda b, i: (0, 0))],
        out_specs=row_spec,
        out_shape=jax.ShapeDtypeStruct((B, S, D), F32),
        compiler_params=_params("parallel", "parallel"),
        name="moe_output_norm",
    )(gathered.reshape(TOP_K, B, S, half), wts, h, s_gate, s_up, s_down, x,
      gate.reshape(B, 1, D), ln_g.reshape(1, D), ln_b.reshape(1, D))


def _moe_layer(x, shift, scale, gate, router_w, router_bias, layer, w_gate, w_up, w_down,
               s_gate, s_up, s_down, ln_g, ln_b):
    B, S, D = x.shape
    T = B * S
    E = router_w.shape[1]
    blk = EXPERT_BLOCK
    h, idx, wts, pos, cnt = _router(x, shift, scale, router_w, router_bias)

    counts = cnt[:, 0].astype(I32)
    padded = (counts + blk - 1) // blk * blk
    pend = jnp.cumsum(padded)
    pstart = pend - padded
    n_blocks = -(-T * TOP_K // blk) + E
    onehot = idx[..., None] == jnp.arange(E, dtype=I32)
    dest = jnp.sum(jnp.where(onehot, pstart, 0), axis=-1) + pos
    block_expert = jnp.minimum(
        jnp.sum((jnp.arange(n_blocks, dtype=I32)[:, None] * blk >= pend[None, :]).astype(I32), axis=1),
        E - 1).astype(I32)
    n_used = (pend[-1:] // blk).astype(I32)

    rows = _sc_dispatch(h.reshape(T, D // 2), dest, n_blocks * blk)
    out_rows = _experts(rows, block_expert, n_used, layer, w_gate, w_up, w_down)
    gathered = _sc_gather(out_rows, dest.transpose(1, 0, 2).reshape(TOP_K * T))
    return _moe_output_norm(gathered, wts.transpose(0, 2, 1), h, layer, s_gate, s_up, s_down, x, gate, ln_g, ln_b)


def kernel(x, c, fox_w_in, fox_b_f, fox_w_o, sb_w_in, sb_w_o, ada_w, ada_b, ln_attn_g, ln_attn_b,
           router_w, router_bias, exp_w_gate, exp_w_up, exp_w_down,
           shared_w_gate, shared_w_up, shared_w_down, ln_ffn_g, ln_ffn_b):
    B, S, D = x.shape
    H = D // HEAD_DIM
    depth = ada_w.shape[0]
    mod = _ada_mod(c, ada_w, ada_b)
    xs = [x[b:b + 1] for b in range(B)]
    for i in range(depth):
        j = i // 2
        w_in = fox_w_in[j] if i % 2 == 0 else sb_w_in[j]
        wq = (w_in[:, :D] * ATTN_SCALE).astype(BF16)
        wk = w_in[:, D:2 * D].astype(BF16)
        wvt = w_in[:, 2 * D:3 * D].T.astype(BF16)
        if i % 2 == 0:
            wf = jnp.zeros((D, LANES), F32).at[:, :H].set(w_in[:, 3 * D:]).astype(BF16)
            bf = jnp.zeros((1, LANES), F32).at[0, :H].set(fox_b_f[j])
            w_o = fox_w_o[j].astype(BF16)
        else:
            w_o = sb_w_o[j].astype(BF16)
        for b in range(B):
            xb = xs[b]
            sh_a, sc_a, g_a, sh_m, sc_m, g_m = jnp.split(mod[i, b:b + 1], 6, axis=-1)
            if i % 2 == 0:
                q, k_aug, vt, fcum, stats = _in_projection_forget(xb, sh_a, sc_a, wq, wk, wvt, wf, bf)
                o = _fox_attention(q, k_aug, vt, fcum, stats, H)
            else:
                q, k, vt = _in_projection(xb, sh_a, sc_a, wq, wk, wvt)
                o = _sb_attention(q, k, vt, H)
            xb = _out_projection_norm(o, w_o, xb, 1.0 + g_a, ln_attn_g[i], ln_attn_b[i])
            xs[b] = _moe_layer(xb, sh_m, sc_m, 1.0 + g_m, router_w[i], router_bias[i], i,
                               exp_w_gate, exp_w_up, exp_w_down,
                               shared_w_gate, shared_w_up, shared_w_down, ln_ffn_g[i], ln_ffn_b[i])
    return jnp.concatenate(xs, axis=0)
```

```python
import functools
import math

import jax
import jax.numpy as jnp
from jax import lax
from jax.experimental import pallas as pl
from jax.experimental.pallas import tpu as pltpu
from jax.experimental.pallas import tpu_sc as plsc

F32 = jnp.float32
BF16 = jnp.bfloat16
I32 = jnp.int32
U32 = jnp.uint32

HEAD_DIM = 128
TOP_K = 8
ROUTED_SCALE = 2.5
LN_EPS = 1e-5
DEPTH = 2
DEEPNORM_ALPHA = (2 * DEPTH) ** 0.25
ATTN_SCALE = 1.0 / math.sqrt(HEAD_DIM)

LANES = 128
UNDERFLOW_LOG = 105.0
VMEM_LIMIT = 56 * 1024 * 1024

ROW_TILE = 512
ATTN_TILE = 512
SB_TILE = 256
SB_HEADS_PER_STEP = 2
EXPERT_BLOCK = 512
SC_WINDOW = 64
SC_SUBCORES = 16
SC_WORKERS = 2 * SC_SUBCORES
N_BIAS_COLS = 3
BOUND_SLACK = 1.008

NT_DIMS = (((1,), (1,)), ((), ()))


def _params(*sem):
    return pltpu.CompilerParams(dimension_semantics=sem, vmem_limit_bytes=VMEM_LIMIT)


def _log_sigmoid(z):
    return jnp.minimum(z, 0.0) - jnp.log1p(jnp.exp(-jnp.abs(z)))


def _split3(x):
    hi = x.astype(BF16)
    r = x - hi.astype(F32)
    mid = r.astype(BF16)
    lo = (r - mid.astype(F32)).astype(BF16)
    return hi, mid, lo


def _pack_halves(x):
    n = x.shape[1] // 2
    bits = lax.bitcast_convert_type(x.astype(BF16).astype(F32), U32)
    return (bits[:, :n] >> 16) | (bits[:, n:] & jnp.uint32(0xFFFF0000))


def _unpack_halves(w):
    lo = lax.bitcast_convert_type(w << 16, F32).astype(BF16)
    hi = lax.bitcast_convert_type(w & jnp.uint32(0xFFFF0000), F32).astype(BF16)
    return lo, hi


def _layer_norm(z, g, b):
    mu = jnp.mean(z, axis=-1, keepdims=True)
    d = z - mu
    var = jnp.mean(d * d, axis=-1, keepdims=True)
    return d * lax.rsqrt(var + LN_EPS) * g + b


def _ada_kernel(c_ref, w_ref, b_ref, o_ref):
    c = c_ref[...]
    act = c * jax.nn.sigmoid(c)
    o_ref[...] = jnp.dot(act, w_ref[...], preferred_element_type=F32,
                         precision=lax.Precision.HIGHEST) + b_ref[...]


def _ada_mod(c, ada_w, ada_b):
    B, D = c.shape
    L, _, N = ada_w.shape
    cp = jnp.zeros((8, D), F32).at[:B].set(c)
    out = pl.pallas_call(
        _ada_kernel,
        grid=(L, N // D),
        in_specs=[pl.BlockSpec((8, D), lambda l, j: (0, 0)),
                  pl.BlockSpec((None, D, D), lambda l, j: (l, 0, j)),
                  pl.BlockSpec((None, 1, D), lambda l, j: (l, 0, j))],
        out_specs=pl.BlockSpec((None, 8, D), lambda l, j: (l, 0, j)),
        out_shape=jax.ShapeDtypeStruct((L, 8, N), F32),
        compiler_params=_params("parallel", "parallel"),
        name="ada_mod",
    )(cp, ada_w, ada_b.reshape(L, 1, N))
    return out[:, :B]


def _project_qkv(h, wq_ref, wk_ref, wvt_ref, q_ref, k_ref, vt_ref):
    q_ref[...] = jnp.dot(h, wq_ref[...], preferred_element_type=F32).astype(BF16)
    k_ref[...] = jnp.dot(h, wk_ref[...], preferred_element_type=F32).astype(BF16)
    vt_ref[...] = lax.dot_general(wvt_ref[...], h, NT_DIMS, preferred_element_type=F32).astype(BF16)


def _inproj_kernel(x_ref, sh_ref, sc_ref, wq_ref, wk_ref, wvt_ref, q_ref, k_ref, vt_ref):
    h = (x_ref[...] * (1.0 + sc_ref[...]) + sh_ref[...]).astype(BF16)
    _project_qkv(h, wq_ref, wk_ref, wvt_ref, q_ref, k_ref, vt_ref)


def _inproj_forget_kernel(x_ref, sh_ref, sc_ref, wq_ref, wk_ref, wvt_ref, wf_ref, bf_ref, sel_ref, hsum_ref,
                          q_ref, ka_ref, vt_ref, f_ref, stats_ref, carry_ref):
    tm, D = x_ref.shape
    n_heads = D // HEAD_DIM
    h = (x_ref[...] * (1.0 + sc_ref[...]) + sh_ref[...]).astype(BF16)
    q = jnp.dot(h, wq_ref[...], preferred_element_type=F32).astype(BF16)
    k = jnp.dot(h, wk_ref[...], preferred_element_type=F32).astype(BF16)
    q_ref[...] = q
    vt_ref[...] = lax.dot_general(wvt_ref[...], h, NT_DIMS, preferred_element_type=F32).astype(BF16)

    @pl.when(pl.program_id(1) == 0)
    def _():
        carry_ref[...] = jnp.zeros_like(carry_ref)

    logf = _log_sigmoid(jnp.dot(h, wf_ref[...], preferred_element_type=F32) + bf_ref[...])
    row = lax.broadcasted_iota(I32, (tm, tm), 0)
    col = lax.broadcasted_iota(I32, (tm, tm), 1)
    tri = jnp.where(row >= col, 1.0, 0.0).astype(BF16)
    hi, mid, lo = _split3(logf)
    cs = (jnp.dot(tri, hi, preferred_element_type=F32)
          + jnp.dot(tri, mid, preferred_element_type=F32)
          + jnp.dot(tri, lo, preferred_element_type=F32))
    cum = cs + carry_ref[...]
    f_ref[...] = cum
    carry_ref[...] = cum[tm - 1:tm, :]

    pieces = jnp.concatenate(_split3(-cum), axis=1)
    for g in range(n_heads):
        ka_ref[:, 2 * g * LANES:(2 * g + 1) * LANES] = k[:, g * HEAD_DIM:(g + 1) * HEAD_DIM]
    for pair in range(n_heads // 2):
        moved = jnp.dot(pieces, sel_ref[pair], preferred_element_type=F32).astype(BF16)
        for half in range(2):
            g = 2 * pair + half
            ka_ref[:, (2 * g + 1) * LANES:(2 * g + 2) * LANES] = moved[:, half * LANES:(half + 1) * LANES]

    qf = q.astype(F32)
    kf = k.astype(F32)
    stats_ref[...] = (jnp.dot((qf * qf).astype(BF16), hsum_ref[0], preferred_element_type=F32)
                      + jnp.dot((kf * kf).astype(BF16), hsum_ref[1], preferred_element_type=F32)
                      + jnp.dot((qf * kf).astype(BF16), hsum_ref[2], preferred_element_type=F32))


def _in_projection(x, shift, scale, wq, wk, wvt):
    B, S, D = x.shape
    tm = min(ROW_TILE, S)
    row_spec = pl.BlockSpec((None, tm, D), lambda b, i: (b, i, 0))
    vec_spec = pl.BlockSpec((None, 1, D), lambda b, i: (b, 0, 0))
    w_spec = pl.BlockSpec((D, D), lambda b, i: (0, 0))
    return pl.pallas_call(
        _inproj_kernel,
        grid=(B, S // tm),
        in_specs=[row_spec, vec_spec, vec_spec, w_spec, w_spec, w_spec],
        out_specs=[row_spec, row_spec, pl.BlockSpec((None, D, tm), lambda b, i: (b, 0, i))],
        out_shape=[jax.ShapeDtypeStruct((B, S, D), BF16), jax.ShapeDtypeStruct((B, S, D), BF16),
                   jax.ShapeDtypeStruct((B, D, S), BF16)],
        compiler_params=_params("parallel", "parallel"),
        name="in_projection",
    )(x, shift.reshape(B, 1, D), scale.reshape(B, 1, D), wq, wk, wvt)


def _in_projection_forget(x, x_batch, shift, scale, wq, wk, wvt, wf, bf):
    _, S, D = x.shape
    B = shift.shape[0]
    n_heads = D // HEAD_DIM
    tm = min(ROW_TILE, S)
    x_spec = pl.BlockSpec((None, tm, D), lambda b, i: (b + x_batch, i, 0))
    row_spec = pl.BlockSpec((None, tm, D), lambda b, i: (b, i, 0))
    vec_spec = pl.BlockSpec((None, 1, D), lambda b, i: (b, 0, 0))
    w_spec = pl.BlockSpec((D, D), lambda b, i: (0, 0))
    lane_spec = pl.BlockSpec((None, tm, LANES), lambda b, i: (b, i, 0))
    heads = jnp.arange(n_heads)
    sel = jnp.zeros((n_heads // 2, N_BIAS_COLS * LANES, 2 * LANES), BF16)
    for p in range(N_BIAS_COLS):
        sel = sel.at[heads // 2, p * LANES + heads, (heads % 2) * LANES + p].set(1.0)
    feat = jnp.arange(D)
    hsum = jnp.zeros((3, D, LANES), BF16)
    for c in range(3):
        hsum = hsum.at[c, feat, c * n_heads + feat // HEAD_DIM].set(1.0)
    return pl.pallas_call(
        _inproj_forget_kernel,
        grid=(B, S // tm),
        in_specs=[x_spec, vec_spec, vec_spec, w_spec, w_spec, w_spec,
                  pl.BlockSpec((D, LANES), lambda b, i: (0, 0)),
                  pl.BlockSpec((1, LANES), lambda b, i: (0, 0)),
                  pl.BlockSpec((n_heads // 2, N_BIAS_COLS * LANES, 2 * LANES), lambda b, i: (0, 0, 0)),
                  pl.BlockSpec((3, D, LANES), lambda b, i: (0, 0, 0))],
        out_specs=[row_spec, pl.BlockSpec((None, tm, 2 * D), lambda b, i: (b, i, 0)),
                   pl.BlockSpec((None, D, tm), lambda b, i: (b, 0, i)), lane_spec, lane_spec],
        out_shape=[jax.ShapeDtypeStruct((B, S, D), BF16), jax.ShapeDtypeStruct((B, S, 2 * D), BF16),
                   jax.ShapeDtypeStruct((B, D, S), BF16), jax.ShapeDtypeStruct((B, S, LANES), F32),
                   jax.ShapeDtypeStruct((B, S, LANES), F32)],
        scratch_shapes=[pltpu.VMEM((1, LANES), F32)],
        compiler_params=_params("parallel", "arbitrary"),
        name="in_projection_forget",
    )(x, shift.reshape(B, 1, D), scale.reshape(B, 1, D), wq, wk, wvt, wf, bf, sel, hsum)


def _fox_kernel(js_ref, q_ref, ka_ref, vt_ref, o_ref, m_ref, l_ref, acc_ref, s_ref, *, n_heads):
    t = q_ref.shape[0]
    b, h, i = pl.program_id(0), pl.program_id(1), pl.program_id(2)
    nq = pl.num_programs(2)
    m_ref[...] = jnp.full(m_ref.shape, -jnp.inf, F32)
    l_ref[...] = jnp.zeros_like(l_ref)
    acc_ref[...] = jnp.zeros_like(acc_ref)
    sub = lax.broadcasted_iota(I32, (LANES, t), 0)
    q_aug_t = jnp.concatenate([q_ref[...].astype(F32).T.astype(BF16),
                               jnp.where(sub < N_BIAS_COLS, 1.0, 0.0).astype(BF16)], axis=0)

    def scores(j, slot):
        ks = pl.multiple_of(j * t, t)
        s_ref[slot] = jnp.dot(ka_ref[pl.ds(ks, t), :], q_aug_t, preferred_element_type=F32)

    def accumulate(j, slot, masked):
        ks = pl.multiple_of(j * t, t)
        s = s_ref[slot]
        if masked:
            key = lax.broadcasted_iota(I32, (t, t), 0)
            qry = lax.broadcasted_iota(I32, (t, t), 1)
            s = jnp.where(key <= qry, s, -jnp.inf)
        m_prev = m_ref[...]
        m_new = jnp.maximum(m_prev, jnp.max(s, axis=0, keepdims=True))
        alpha = jnp.exp(m_prev - m_new)
        p = jnp.exp(s - m_new)
        l_ref[...] = alpha * l_ref[...] + jnp.sum(p, axis=0, keepdims=True)
        acc_ref[...] = alpha * acc_ref[...] + jnp.dot(
            vt_ref[:, pl.ds(ks, t)], p.astype(BF16), preferred_element_type=F32)
        m_ref[...] = m_new

    first = js_ref[(b * n_heads + h) * nq + i]
    n_full = i - first
    scores(first, 0)

    def body(pair, carry):
        j = first + 2 * pair
        scores(j + 1, 1)
        accumulate(j, 0, False)
        scores(j + 2, 0)
        accumulate(j + 1, 1, False)
        return carry

    lax.fori_loop(0, n_full // 2, body, 0)

    @pl.when(n_full % 2 == 1)
    def _():
        scores(i, 1)
        accumulate(i - 1, 0, False)
        accumulate(i, 1, True)

    @pl.when(n_full % 2 == 0)
    def _():
        accumulate(i, 0, True)

    o_ref[...] = (acc_ref[...] / l_ref[...]).T.astype(o_ref.dtype)


def _fox_first_blocks(stats, fcum, n_heads, t):
    B, S, _ = stats.shape
    qn = jnp.sqrt(stats[..., :n_heads])
    kn = jnp.sqrt(jnp.max(stats[..., n_heads:2 * n_heads], axis=1))
    diag = stats[..., 2 * n_heads:3 * n_heads]
    f = fcum[..., :n_heads]
    need = qn * kn[:, None, :] * BOUND_SLACK - diag + f + (UNDERFLOW_LOG + 1.0)
    nq = S // t
    need = jnp.max(need.reshape(B, nq, t, n_heads), axis=2)
    f_end = f.reshape(B, nq, t, n_heads)[:, :, t - 1, :]
    skippable = f_end[:, None, :, :] > need[:, :, None, :]
    first = jnp.sum(skippable.astype(I32), axis=2)
    first = jnp.minimum(first, jnp.arange(nq, dtype=I32)[None, :, None])
    return first.transpose(0, 2, 1).reshape(-1)


def _fox_attention(q, k_aug, vt, fcum, stats, n_heads):
    B, S, D = q.shape
    t = min(ATTN_TILE, S)
    nq = S // t
    first = _fox_first_blocks(stats, fcum, n_heads, t)
    grid_spec = pltpu.PrefetchScalarGridSpec(
        num_scalar_prefetch=1,
        grid=(B, n_heads, nq),
        in_specs=[pl.BlockSpec((None, t, HEAD_DIM), lambda b, h, i, js: (b, i, h)),
                  pl.BlockSpec((None, S, 2 * LANES), lambda b, h, i, js: (b, 0, h)),
                  pl.BlockSpec((None, HEAD_DIM, S), lambda b, h, i, js: (b, h, 0))],
        out_specs=pl.BlockSpec((None, t, HEAD_DIM), lambda b, h, i, js: (b, i, h)),
        scratch_shapes=[pltpu.VMEM((1, t), F32), pltpu.VMEM((1, t), F32),
                        pltpu.VMEM((HEAD_DIM, t), F32), pltpu.VMEM((2, t, t), F32)],
    )
    return pl.pallas_call(
        functools.partial(_fox_kernel, n_heads=n_heads),
        grid_spec=grid_spec,
        out_shape=jax.ShapeDtypeStruct((B, S, D), BF16),
        compiler_params=_params("parallel", "parallel", "arbitrary"),
        name="fox_attention",
    )(first, q, k_aug, vt)


def _sb_kernel(q_ref, k_ref, vt_ref, o_ref, carry_ref, acc_ref):
    t = q_ref.shape[0]
    n_local = q_ref.shape[1] // HEAD_DIM
    i = pl.program_id(2)
    carry_ref[...] = jnp.zeros_like(carry_ref)
    acc_ref[...] = jnp.zeros_like(acc_ref)
    key = lax.broadcasted_iota(I32, (t, t), 0)
    qry = lax.broadcasted_iota(I32, (t, t), 1)
    later = jnp.where(qry > key, 1.0, 0.0).astype(BF16)
    later2 = jnp.concatenate([later, later], axis=1)

    def block(j, masked):
        ks = pl.multiple_of(j * t, t)
        feats = [slice(g * HEAD_DIM, (g + 1) * HEAD_DIM) for g in range(n_local)]
        zs = [lax.dot_general(k_ref[pl.ds(ks, t), f], q_ref[:, f], NT_DIMS, preferred_element_type=F32)
              for f in feats]
        log1ms, suffixes = [], []
        for z in zs:
            log1m = _log_sigmoid(-z)
            if masked:
                log1m = jnp.where(key < qry, log1m, 0.0)
            hi = log1m.astype(BF16)
            mid = (log1m - hi.astype(F32)).astype(BF16)
            log1ms.append(log1m)
            suffixes.append(jnp.dot(later2, jnp.concatenate([hi, mid], axis=0), preferred_element_type=F32))
        for g, f in enumerate(feats):
            la = zs[g] + log1ms[g] + suffixes[g] + carry_ref[g:g + 1, :]
            if masked:
                la = jnp.where(key < qry, la, -jnp.inf)
            a = jnp.exp(la)
            acc_ref[f, :] += jnp.dot(vt_ref[f, pl.ds(ks, t)], a.astype(BF16), preferred_element_type=F32)
            carry_ref[g:g + 1, :] += jnp.sum(log1ms[g], axis=0, keepdims=True)

    block(i, True)

    def cond(state):
        j, top = state
        return jnp.logical_and(j >= 0, top > -UNDERFLOW_LOG)

    def body(state):
        j, _ = state
        block(j, False)
        return j - 1, jnp.max(carry_ref[...])

    lax.while_loop(cond, body, (i - 1, jnp.max(carry_ref[...])))
    for g in range(n_local):
        feat = slice(g * HEAD_DIM, (g + 1) * HEAD_DIM)
        o_ref[:, feat] = acc_ref[feat, :].T.astype(o_ref.dtype)


def _sb_attention(q, k, vt, n_heads):
    B, S, D = q.shape
    t = min(SB_TILE, S)
    width = SB_HEADS_PER_STEP * HEAD_DIM
    return pl.pallas_call(
        _sb_kernel,
        grid=(B, n_heads // SB_HEADS_PER_STEP, S // t),
        in_specs=[pl.BlockSpec((None, t, width), lambda b, h, i: (b, i, h)),
                  pl.BlockSpec((None, S, width), lambda b, h, i: (b, 0, h)),
                  pl.BlockSpec((None, width, S), lambda b, h, i: (b, h, 0))],
        out_specs=pl.BlockSpec((None, t, width), lambda b, h, i: (b, i, h)),
        out_shape=jax.ShapeDtypeStruct((B, S, D), BF16),
        scratch_shapes=[pltpu.VMEM((SB_HEADS_PER_STEP, t), F32), pltpu.VMEM((width, t), F32)],
        compiler_params=_params("parallel", "parallel", "arbitrary"),
        name="sb_attention",
    )(q, k, vt)


def _outproj_ln_kernel(o_ref, w_ref, x_ref, gate_ref, g_ref, b_ref, out_ref):
    y = jnp.dot(o_ref[...], w_ref[...], preferred_element_type=F32)
    z = DEEPNORM_ALPHA * x_ref[...] + gate_ref[...] * y
    out_ref[...] = _layer_norm(z, g_ref[...], b_ref[...])


def _out_projection_norm(o, w_bf16, x, x_batch, gate, ln_g, ln_b):
    B, S, D = o.shape
    tm = min(ROW_TILE, S)
    row_spec = pl.BlockSpec((None, tm, D), lambda b, i: (b, i, 0))
    return pl.pallas_call(
        _outproj_ln_kernel,
        grid=(B, S // tm),
        in_specs=[row_spec,
                  pl.BlockSpec((D, D), lambda b, i: (0, 0)),
                  pl.BlockSpec((None, tm, D), lambda b, i: (b + x_batch, i, 0)),
                  pl.BlockSpec((None, 1, D), lambda b, i: (b, 0, 0)),
                  pl.BlockSpec((1, D), lambda b, i: (0, 0)),
                  pl.BlockSpec((1, D), lambda b, i: (0, 0))],
        out_specs=row_spec,
        out_shape=jax.ShapeDtypeStruct((B, S, D), F32),
        compiler_params=_params("parallel", "parallel"),
        name="out_projection_norm",
    )(o, w_bf16, x, gate.reshape(B, 1, D), ln_g.reshape(1, D), ln_b.reshape(1, D))


def _router_kernel(x_ref, sh_ref, sc_ref, rwt_ref, rb_ref,
                   h_ref, idx_ref, wts_ref, pos_ref, cnt_ref, carry_ref):
    tm = x_ref.shape[0]
    n_experts = rwt_ref.shape[0]

    @pl.when(jnp.logical_and(pl.program_id(0) == 0, pl.program_id(1) == 0))
    def _():
        carry_ref[...] = jnp.zeros_like(carry_ref)

    h = x_ref[...] * (1.0 + sc_ref[...]) + sh_ref[...]
    h_ref[...] = _pack_halves(h)
    logits = lax.dot_general(rwt_ref[...], h, NT_DIMS, preferred_element_type=F32,
                             precision=lax.Precision.HIGHEST)
    scores = jax.nn.sigmoid(logits)
    expert = lax.broadcasted_iota(I32, (n_experts, tm), 0).astype(F32)
    slot = lax.broadcasted_iota(I32, (TOP_K, tm), 0)
    sel = scores + rb_ref[...]

    chosen = jnp.zeros((n_experts, tm), F32)
    idx_acc = jnp.zeros((TOP_K, tm), F32)
    w_acc = jnp.zeros((TOP_K, tm), F32)
    total = jnp.zeros((1, tm), F32)
    picks = []
    for k in range(TOP_K):
        best = jnp.max(sel, axis=0, keepdims=True)
        pick = jnp.min(jnp.where(sel == best, expert, float(n_experts)), axis=0, keepdims=True)
        hit = expert == pick
        score = jnp.sum(jnp.where(hit, scores, 0.0), axis=0, keepdims=True)
        sel = jnp.where(hit, -jnp.inf, sel)
        chosen = jnp.where(hit, 1.0, chosen)
        idx_acc = jnp.where(slot == k, pick, idx_acc)
        w_acc = jnp.where(slot == k, score, w_acc)
        total = total + score
        picks.append(pick)
    w_acc = w_acc / total * ROUTED_SCALE

    row = lax.broadcasted_iota(I32, (tm, tm), 0)
    col = lax.broadcasted_iota(I32, (tm, tm), 1)
    earlier = jnp.where(row < col, 1.0, 0.0).astype(BF16)
    chosen_b = chosen.astype(BF16)
    before = jnp.dot(chosen_b, earlier, preferred_element_type=F32) + carry_ref[...]
    pos_acc = jnp.zeros((TOP_K, tm), F32)
    for k in range(TOP_K):
        rank = jnp.sum(jnp.where(expert == picks[k], before, 0.0), axis=0, keepdims=True)
        pos_acc = jnp.where(slot == k, rank, pos_acc)

    carry_ref[...] += jnp.dot(chosen_b, jnp.ones((tm, tm), BF16), preferred_element_type=F32)
    idx_ref[...] = idx_acc.astype(I32)
    wts_ref[...] = w_acc
    pos_ref[...] = pos_acc.astype(I32)
    cnt_ref[...] = carry_ref[:, :LANES]


def _router(x, shift, scale, router_w, router_bias):
    B, S, D = x.shape
    E = router_w.shape[1]
    tm = min(ROW_TILE, S)
    row_spec = pl.BlockSpec((None, tm, D), lambda b, i: (b, i, 0))
    vec_spec = pl.BlockSpec((None, 1, D), lambda b, i: (b, 0, 0))
    k_spec = pl.BlockSpec((None, TOP_K, tm), lambda b, i: (b, 0, i))
    return pl.pallas_call(
        _router_kernel,
        grid=(B, S // tm),
        in_specs=[row_spec, vec_spec, vec_spec,
                  pl.BlockSpec((E, D), lambda b, i: (0, 0)),
                  pl.BlockSpec((E, tm), lambda b, i: (0, 0))],
        out_specs=[pl.BlockSpec((None, tm, D // 2), lambda b, i: (b, i, 0)), k_spec, k_spec, k_spec,
                   pl.BlockSpec((E, LANES), lambda b, i: (0, 0))],
        out_shape=[jax.ShapeDtypeStruct((B, S, D // 2), U32),
                   jax.ShapeDtypeStruct((B, TOP_K, S), I32),
                   jax.ShapeDtypeStruct((B, TOP_K, S), F32),
                   jax.ShapeDtypeStruct((B, TOP_K, S), I32),
                   jax.ShapeDtypeStruct((E, LANES), F32)],
        scratch_shapes=[pltpu.VMEM((E, tm), F32)],
        compiler_params=_params("arbitrary", "arbitrary"),
        name="router",
    )(x, shift.reshape(B, 1, D), scale.reshape(B, 1, D), router_w.T,
      jnp.broadcast_to(router_bias[:, None], (E, tm)))


def _sc_mesh():
    return plsc.VectorSubcoreMesh(core_axis_name="core", subcore_axis_name="subcore")


def _sc_dispatch(h, dest, n_rows):
    T, D = h.shape
    B, K, S = dest.shape
    W = SC_WINDOW
    per_worker = T // SC_WORKERS
    dest_w = dest.reshape(B, K, S // W, W).transpose(0, 2, 1, 3).reshape(T * K)

    @pl.kernel(out_type=jax.ShapeDtypeStruct((n_rows, D), h.dtype), mesh=_sc_mesh(),
               scratch_types=[pltpu.VMEM((K * W,), I32), pltpu.VMEM((W, D), h.dtype)])
    def dispatch(h_hbm, idx_hbm, rows_hbm, idx_v, buf_v):
        worker = lax.axis_index("core") * SC_SUBCORES + lax.axis_index("subcore")

        @pl.loop(0, per_worker // W)
        def _(s):
            base = pl.multiple_of(worker * per_worker + s * W, W)
            pltpu.sync_copy(idx_hbm.at[pl.ds(base * K, K * W)], idx_v)
            pltpu.sync_copy(h_hbm.at[pl.ds(base, W)], buf_v)
            for k in range(K):
                pltpu.sync_copy(buf_v, rows_hbm.at[idx_v.at[pl.ds(k * W, W)]])

    return dispatch(h, dest_w)


def _sc_gather(rows, index):
    R = index.shape[0]
    D = rows.shape[1]
    W = SC_WINDOW
    per_worker = R // SC_WORKERS

    @pl.kernel(out_type=jax.ShapeDtypeStruct((R, D), rows.dtype), mesh=_sc_mesh(),
               scratch_types=[pltpu.VMEM((W,), I32), pltpu.VMEM((W, D), rows.dtype)])
    def gather(rows_hbm, idx_hbm, out_hbm, idx_v, buf_v):
        worker = lax.axis_index("core") * SC_SUBCORES + lax.axis_index("subcore")

        @pl.loop(0, per_worker // W)
        def _(s):
            base = pl.multiple_of(worker * per_worker + s * W, W)
            pltpu.sync_copy(idx_hbm.at[pl.ds(base, W)], idx_v)
            pltpu.sync_copy(rows_hbm.at[idx_v], buf_v)
            pltpu.sync_copy(buf_v, out_hbm.at[pl.ds(base, W)])

    return gather(rows, index)


def _swiglu_packed(x_packed, w_gate, w_up, w_down):
    half = x_packed.shape[1]
    lo, hi = _unpack_halves(x_packed)
    wg = w_gate.astype(BF16)
    wu = w_up.astype(BF16)
    g = (jnp.dot(lo, wg[:half, :], preferred_element_type=F32)
         + jnp.dot(hi, wg[half:, :], preferred_element_type=F32))
    u = (jnp.dot(lo, wu[:half, :], preferred_element_type=F32)
         + jnp.dot(hi, wu[half:, :], preferred_element_type=F32))
    a = (g * jax.nn.sigmoid(g) * u).astype(BF16)
    return jnp.dot(a, w_down.astype(BF16), preferred_element_type=F32)


def _expert_kernel(be_ref, used_ref, x_ref, wg_ref, wu_ref, wd_ref, o_ref):
    @pl.when(pl.program_id(0) < used_ref[0])
    def _():
        o_ref[...] = _pack_halves(_swiglu_packed(x_ref[...], wg_ref[...], wu_ref[...], wd_ref[...]))


def _experts(rows, block_expert, n_used, layer, w_gate, w_up, w_down):
    R, half = rows.shape
    D = 2 * half
    F = w_gate.shape[3]
    blk = EXPERT_BLOCK
    grid_spec = pltpu.PrefetchScalarGridSpec(
        num_scalar_prefetch=2,
        grid=(R // blk,),
        in_specs=[pl.BlockSpec((blk, half), lambda r, be, nu: (r, 0)),
                  pl.BlockSpec((None, None, D, F), lambda r, be, nu: (layer, be[r], 0, 0)),
                  pl.BlockSpec((None, None, D, F), lambda r, be, nu: (layer, be[r], 0, 0)),
                  pl.BlockSpec((None, None, F, D), lambda r, be, nu: (layer, be[r], 0, 0))],
        out_specs=pl.BlockSpec((blk, half), lambda r, be, nu: (r, 0)),
    )
    return pl.pallas_call(
        _expert_kernel,
        grid_spec=grid_spec,
        out_shape=jax.ShapeDtypeStruct((R, half), U32),
        compiler_params=_params("arbitrary"),
        name="routed_experts",
    )(block_expert, n_used, rows, w_gate, w_up, w_down)


def _moe_out_kernel(ga_ref, w_ref, h_ref, sg_ref, su_ref, sd_ref, x_ref, gate_ref, g_ref, b_ref, out_ref):
    half = h_ref.shape[1]
    y = _swiglu_packed(h_ref[...], sg_ref[...], su_ref[...], sd_ref[...])
    w = w_ref[...]
    y_lo = y[:, :half]
    y_hi = y[:, half:]
    for k in range(TOP_K):
        packed = ga_ref[k]
        wk = w[:, k:k + 1]
        y_lo = y_lo + wk * lax.bitcast_convert_type(packed << 16, F32)
        y_hi = y_hi + wk * lax.bitcast_convert_type(packed & jnp.uint32(0xFFFF0000), F32)
    y = jnp.concatenate([y_lo, y_hi], axis=1)
    z = DEEPNORM_ALPHA * x_ref[...] + gate_ref[...] * y
    out_ref[...] = _layer_norm(z, g_ref[...], b_ref[...])


def _moe_out_into_kernel(prev_ref, *refs):
    del prev_ref
    _moe_out_kernel(*refs)


def _moe_output_norm(gathered, wts, h, layer, s_gate, s_up, s_down, x, gate, ln_g, ln_b, out_batch, out_total, prev):
    B, S, D = x.shape
    half = D // 2
    F = s_gate.shape[2]
    tm = min(ROW_TILE // 2, S)
    row_spec = pl.BlockSpec((None, tm, D), lambda b, i: (b, i, 0))
    carried = () if prev is None else (prev,)
    return pl.pallas_call(
        _moe_out_kernel if prev is None else _moe_out_into_kernel,
        grid=(B, S // tm),
        in_specs=[pl.BlockSpec(memory_space=pl.ANY)] * len(carried)
        + [pl.BlockSpec((TOP_K, None, tm, half), lambda b, i: (0, b, i, 0)),
                  pl.BlockSpec((None, tm, TOP_K), lambda b, i: (b, i, 0)),
                  pl.BlockSpec((None, tm, half), lambda b, i: (b, i, 0)),
                  pl.BlockSpec((None, D, F), lambda b, i: (layer, 0, 0)),
                  pl.BlockSpec((None, D, F), lambda b, i: (layer, 0, 0)),
                  pl.BlockSpec((None, F, D), lambda b, i: (layer, 0, 0)),
                  row_spec,
                  pl.BlockSpec((None, 1, D), lambda b, i: (b, 0, 0)),
                  pl.BlockSpec((1, D), lambda b, i: (0, 0)),
                  pl.BlockSpec((1, D), lambda b, i: (0, 0))],
        out_specs=pl.BlockSpec((None, tm, D), lambda b, i: (b + out_batch, i, 0)),
        out_shape=jax.ShapeDtypeStruct((out_total, S, D), F32),
        input_output_aliases={} if prev is None else {0: 0},
        compiler_params=_params("parallel", "parallel"),
        name="moe_output_norm",
    )(*carried, gathered.reshape(TOP_K, B, S, half), wts, h, s_gate, s_up, s_down, x,
      gate.reshape(B, 1, D), ln_g.reshape(1, D), ln_b.reshape(1, D))


def _moe_layer(x, shift, scale, gate, router_w, router_bias, layer, w_gate, w_up, w_down,
               s_gate, s_up, s_down, ln_g, ln_b, out_batch=0, out_total=None, prev=None):
    B, S, D = x.shape
    T = B * S
    E = router_w.shape[1]
    blk = EXPERT_BLOCK
    h, idx, wts, pos, cnt = _router(x, shift, scale, router_w, router_bias)

    counts = cnt[:, 0].astype(I32)
    padded = (counts + blk - 1) // blk * blk
    pend = jnp.cumsum(padded)
    pstart = pend - padded
    n_blocks = -(-T * TOP_K // blk) + E
    onehot = idx[..., None] == jnp.arange(E, dtype=I32)
    dest = jnp.sum(jnp.where(onehot, pstart, 0), axis=-1) + pos
    block_expert = jnp.minimum(
        jnp.sum((jnp.arange(n_blocks, dtype=I32)[:, None] * blk >= pend[None, :]).astype(I32), axis=1),
        E - 1).astype(I32)
    n_used = (pend[-1:] // blk).astype(I32)

    rows = _sc_dispatch(h.reshape(T, D // 2), dest, n_blocks * blk)
    out_rows = _experts(rows, block_expert, n_used, layer, w_gate, w_up, w_down)
    gathered = _sc_gather(out_rows, dest.transpose(1, 0, 2).reshape(TOP_K * T))
    return _moe_output_norm(gathered, wts.transpose(0, 2, 1), h, layer, s_gate, s_up, s_down, x, gate, ln_g, ln_b,
                            out_batch, B if out_total is None else out_total, prev)


def kernel(x, c, fox_w_in, fox_b_f, fox_w_o, sb_w_in, sb_w_o, ada_w, ada_b, ln_attn_g, ln_attn_b,
           router_w, router_bias, exp_w_gate, exp_w_up, exp_w_down,
           shared_w_gate, shared_w_up, shared_w_down, ln_ffn_g, ln_ffn_b):
    B, S, D = x.shape
    H = D // HEAD_DIM
    depth = ada_w.shape[0]
    mod = _ada_mod(c, ada_w, ada_b)
    xs = [None] * B
    out = None
    for i in range(depth):
        j = i // 2
        w_in = fox_w_in[j] if i % 2 == 0 else sb_w_in[j]
        wq = (w_in[:, :D] * ATTN_SCALE).astype(BF16)
        wk = w_in[:, D:2 * D].astype(BF16)
        wvt = w_in[:, 2 * D:3 * D].T.astype(BF16)
        if i % 2 == 0:
            wf = jnp.zeros((D, LANES), F32).at[:, :H].set(w_in[:, 3 * D:]).astype(BF16)
            bf = jnp.zeros((1, LANES), F32).at[0, :H].set(fox_b_f[j])
            w_o = fox_w_o[j].astype(BF16)
        else:
            w_o = sb_w_o[j].astype(BF16)
        for b in range(B):
            xb, x_batch = (x, b) if i == 0 else (xs[b], 0)
            sh_a, sc_a, g_a, sh_m, sc_m, g_m = jnp.split(mod[i, b:b + 1], 6, axis=-1)
            if i % 2 == 0:
                q, k_aug, vt, fcum, stats = _in_projection_forget(xb, x_batch, sh_a, sc_a, wq, wk, wvt, wf, bf)
                o = _fox_attention(q, k_aug, vt, fcum, stats, H)
            else:
                q, k, vt = _in_projection(xb, sh_a, sc_a, wq, wk, wvt)
                o = _sb_attention(q, k, vt, H)
            xb = _out_projection_norm(o, w_o, xb, x_batch, 1.0 + g_a, ln_attn_g[i], ln_attn_b[i])
            moe_args = (xb, sh_m, sc_m, 1.0 + g_m, router_w[i], router_bias[i], i, exp_w_gate, exp_w_up, exp_w_down,
                        shared_w_gate, shared_w_up, shared_w_down, ln_ffn_g[i], ln_ffn_b[i])
            if i + 1 < depth:
                xs[b] = _moe_layer(*moe_args)
            else:
                out = _moe_layer(*moe_args, out_batch=b, out_total=B, prev=out)
    return out
```

```python
import functools
import math

import jax
import jax.numpy as jnp
from jax import lax
from jax.experimental import pallas as pl
from jax.experimental.pallas import tpu as pltpu
from jax.experimental.pallas import tpu_sc as plsc

F32 = jnp.float32
BF16 = jnp.bfloat16
I32 = jnp.int32
U32 = jnp.uint32

HEAD_DIM = 128
TOP_K = 8
ROUTED_SCALE = 2.5
LN_EPS = 1e-5
DEPTH = 2
DEEPNORM_ALPHA = (2 * DEPTH) ** 0.25
ATTN_SCALE = 1.0 / math.sqrt(HEAD_DIM)
LOG2_E = math.log2(math.e)

LANES = 128
UNDERFLOW_LOG = 105.0
VMEM_LIMIT = 56 * 1024 * 1024

ROW_TILE = 512
ATTN_TILE = 512
SB_TILE = 256
SB_HEADS_PER_STEP = 2
EXPERT_BLOCK = 512
SC_WINDOW = 64
SC_SUBCORES = 16
SC_WORKERS = 2 * SC_SUBCORES
N_BIAS_COLS = 3
BOUND_SLACK = 1.008

NT_DIMS = (((1,), (1,)), ((), ()))


def _params(*sem):
    return pltpu.CompilerParams(dimension_semantics=sem, vmem_limit_bytes=VMEM_LIMIT)


def _log_sigmoid(z):
    return jnp.minimum(z, 0.0) - jnp.log1p(jnp.exp(-jnp.abs(z)))


def _split3(x):
    hi = x.astype(BF16)
    r = x - hi.astype(F32)
    mid = r.astype(BF16)
    lo = (r - mid.astype(F32)).astype(BF16)
    return hi, mid, lo


def _pack_halves(x):
    n = x.shape[1] // 2
    bits = lax.bitcast_convert_type(x.astype(BF16).astype(F32), U32)
    return (bits[:, :n] >> 16) | (bits[:, n:] & jnp.uint32(0xFFFF0000))


def _unpack_halves(w):
    lo = lax.bitcast_convert_type(w << 16, F32).astype(BF16)
    hi = lax.bitcast_convert_type(w & jnp.uint32(0xFFFF0000), F32).astype(BF16)
    return lo, hi


def _layer_norm(z, g, b):
    mu = jnp.mean(z, axis=-1, keepdims=True)
    d = z - mu
    var = jnp.mean(d * d, axis=-1, keepdims=True)
    return d * lax.rsqrt(var + LN_EPS) * g + b


def _ada_kernel(c_ref, w_ref, b_ref, o_ref):
    c = c_ref[...]
    act = c * jax.nn.sigmoid(c)
    o_ref[...] = jnp.dot(act, w_ref[...], preferred_element_type=F32,
                         precision=lax.Precision.HIGHEST) + b_ref[...]


def _ada_mod(c, ada_w, ada_b):
    B, D = c.shape
    L, _, N = ada_w.shape
    cp = jnp.zeros((8, D), F32).at[:B].set(c)
    out = pl.pallas_call(
        _ada_kernel,
        grid=(L, N // D),
        in_specs=[pl.BlockSpec((8, D), lambda l, j: (0, 0)),
                  pl.BlockSpec((None, D, D), lambda l, j: (l, 0, j)),
                  pl.BlockSpec((None, 1, D), lambda l, j: (l, 0, j))],
        out_specs=pl.BlockSpec((None, 8, D), lambda l, j: (l, 0, j)),
        out_shape=jax.ShapeDtypeStruct((L, 8, N), F32),
        compiler_params=_params("parallel", "parallel"),
        name="ada_mod",
    )(cp, ada_w, ada_b.reshape(L, 1, N))
    return out[:, :B]


def _project_qkv(h, wq_ref, wk_ref, wvt_ref, q_ref, k_ref, vt_ref):
    q_ref[...] = jnp.dot(h, wq_ref[...], preferred_element_type=F32).astype(BF16)
    k_ref[...] = jnp.dot(h, wk_ref[...], preferred_element_type=F32).astype(BF16)
    vt_ref[...] = lax.dot_general(wvt_ref[...], h, NT_DIMS, preferred_element_type=F32).astype(BF16)


def _inproj_kernel(x_ref, sh_ref, sc_ref, wq_ref, wk_ref, wvt_ref, q_ref, k_ref, vt_ref):
    h = (x_ref[...] * (1.0 + sc_ref[...]) + sh_ref[...]).astype(BF16)
    _project_qkv(h, wq_ref, wk_ref, wvt_ref, q_ref, k_ref, vt_ref)


def _inproj_forget_kernel(x_ref, sh_ref, sc_ref, wq_ref, wk_ref, wvt_ref, wf_ref, bf_ref, sel_ref, hsum_ref,
                          q_ref, ka_ref, vt_ref, f_ref, stats_ref, carry_ref):
    tm, D = x_ref.shape
    n_heads = D // HEAD_DIM
    h = (x_ref[...] * (1.0 + sc_ref[...]) + sh_ref[...]).astype(BF16)
    q = jnp.dot(h, wq_ref[...], preferred_element_type=F32).astype(BF16)
    k = jnp.dot(h, wk_ref[...], preferred_element_type=F32).astype(BF16)
    q_ref[...] = q
    vt_ref[...] = lax.dot_general(wvt_ref[...], h, NT_DIMS, preferred_element_type=F32).astype(BF16)

    @pl.when(pl.program_id(1) == 0)
    def _():
        carry_ref[...] = jnp.zeros_like(carry_ref)

    logf = _log_sigmoid(jnp.dot(h, wf_ref[...], preferred_element_type=F32) + bf_ref[...])
    row = lax.broadcasted_iota(I32, (tm, tm), 0)
    col = lax.broadcasted_iota(I32, (tm, tm), 1)
    tri = jnp.where(row >= col, 1.0, 0.0).astype(BF16)
    hi, mid, lo = _split3(logf)
    cs = (jnp.dot(tri, hi, preferred_element_type=F32)
          + jnp.dot(tri, mid, preferred_element_type=F32)
          + jnp.dot(tri, lo, preferred_element_type=F32))
    cum = cs + carry_ref[...]
    f_ref[...] = cum
    carry_ref[...] = cum[tm - 1:tm, :]

    pieces = jnp.concatenate(_split3(-LOG2_E * cum), axis=1)
    for g in range(n_heads):
        ka_ref[:, 2 * g * LANES:(2 * g + 1) * LANES] = k[:, g * HEAD_DIM:(g + 1) * HEAD_DIM]
    for pair in range(n_heads // 2):
        moved = jnp.dot(pieces, sel_ref[pair], preferred_element_type=F32).astype(BF16)
        for half in range(2):
            g = 2 * pair + half
            ka_ref[:, (2 * g + 1) * LANES:(2 * g + 2) * LANES] = moved[:, half * LANES:(half + 1) * LANES]

    qf = q.astype(F32)
    kf = k.astype(F32)
    stats_ref[...] = (jnp.dot((qf * qf).astype(BF16), hsum_ref[0], preferred_element_type=F32)
                      + jnp.dot((kf * kf).astype(BF16), hsum_ref[1], preferred_element_type=F32)
                      + jnp.dot((qf * kf).astype(BF16), hsum_ref[2], preferred_element_type=F32))


def _in_projection(x, shift, scale, wq, wk, wvt):
    B, S, D = x.shape
    tm = min(ROW_TILE, S)
    row_spec = pl.BlockSpec((None, tm, D), lambda b, i: (b, i, 0))
    vec_spec = pl.BlockSpec((None, 1, D), lambda b, i: (b, 0, 0))
    w_spec = pl.BlockSpec((D, D), lambda b, i: (0, 0))
    return pl.pallas_call(
        _inproj_kernel,
        grid=(B, S // tm),
        in_specs=[row_spec, vec_spec, vec_spec, w_spec, w_spec, w_spec],
        out_specs=[row_spec, row_spec, pl.BlockSpec((None, D, tm), lambda b, i: (b, 0, i))],
        out_shape=[jax.ShapeDtypeStruct((B, S, D), BF16), jax.ShapeDtypeStruct((B, S, D), BF16),
                   jax.ShapeDtypeStruct((B, D, S), BF16)],
        compiler_params=_params("parallel", "parallel"),
        name="in_projection",
    )(x, shift.reshape(B, 1, D), scale.reshape(B, 1, D), wq, wk, wvt)


def _in_projection_forget(x, x_batch, shift, scale, wq, wk, wvt, wf, bf):
    _, S, D = x.shape
    B = shift.shape[0]
    n_heads = D // HEAD_DIM
    tm = min(ROW_TILE, S)
    x_spec = pl.BlockSpec((None, tm, D), lambda b, i: (b + x_batch, i, 0))
    row_spec = pl.BlockSpec((None, tm, D), lambda b, i: (b, i, 0))
    vec_spec = pl.BlockSpec((None, 1, D), lambda b, i: (b, 0, 0))
    w_spec = pl.BlockSpec((D, D), lambda b, i: (0, 0))
    lane_spec = pl.BlockSpec((None, tm, LANES), lambda b, i: (b, i, 0))
    heads = jnp.arange(n_heads)
    sel = jnp.zeros((n_heads // 2, N_BIAS_COLS * LANES, 2 * LANES), BF16)
    for p in range(N_BIAS_COLS):
        sel = sel.at[heads // 2, p * LANES + heads, (heads % 2) * LANES + p].set(1.0)
    feat = jnp.arange(D)
    hsum = jnp.zeros((3, D, LANES), BF16)
    for c in range(3):
        hsum = hsum.at[c, feat, c * n_heads + feat // HEAD_DIM].set(1.0)
    return pl.pallas_call(
        _inproj_forget_kernel,
        grid=(B, S // tm),
        in_specs=[x_spec, vec_spec, vec_spec, w_spec, w_spec, w_spec,
                  pl.BlockSpec((D, LANES), lambda b, i: (0, 0)),
                  pl.BlockSpec((1, LANES), lambda b, i: (0, 0)),
                  pl.BlockSpec((n_heads // 2, N_BIAS_COLS * LANES, 2 * LANES), lambda b, i: (0, 0, 0)),
                  pl.BlockSpec((3, D, LANES), lambda b, i: (0, 0, 0))],
        out_specs=[row_spec, pl.BlockSpec((None, tm, 2 * D), lambda b, i: (b, i, 0)),
                   pl.BlockSpec((None, D, tm), lambda b, i: (b, 0, i)), lane_spec, lane_spec],
        out_shape=[jax.ShapeDtypeStruct((B, S, D), BF16), jax.ShapeDtypeStruct((B, S, 2 * D), BF16),
                   jax.ShapeDtypeStruct((B, D, S), BF16), jax.ShapeDtypeStruct((B, S, LANES), F32),
                   jax.ShapeDtypeStruct((B, S, LANES), F32)],
        scratch_shapes=[pltpu.VMEM((1, LANES), F32)],
        compiler_params=_params("parallel", "arbitrary"),
        name="in_projection_forget",
    )(x, shift.reshape(B, 1, D), scale.reshape(B, 1, D), wq, wk, wvt, wf, bf, sel, hsum)


def _fox_kernel(js_ref, q_ref, ka_ref, vt_ref, o_ref, m_ref, l_ref, acc_ref, s_ref, *, n_heads):
    t = q_ref.shape[0]
    b, h, i = pl.program_id(0), pl.program_id(1), pl.program_id(2)
    nq = pl.num_programs(2)
    m_ref[...] = jnp.full(m_ref.shape, -jnp.inf, F32)
    l_ref[...] = jnp.zeros_like(l_ref)
    acc_ref[...] = jnp.zeros_like(acc_ref)
    sub = lax.broadcasted_iota(I32, (LANES, t), 0)
    q_aug_t = jnp.concatenate([q_ref[...].astype(F32).T.astype(BF16),
                               jnp.where(sub < N_BIAS_COLS, 1.0, 0.0).astype(BF16)], axis=0)

    def scores(j, slot):
        ks = pl.multiple_of(j * t, t)
        s_ref[slot] = jnp.dot(ka_ref[pl.ds(ks, t), :], q_aug_t, preferred_element_type=F32)

    def accumulate(j, slot, masked):
        ks = pl.multiple_of(j * t, t)
        s = s_ref[slot]
        if masked:
            key = lax.broadcasted_iota(I32, (t, t), 0)
            qry = lax.broadcasted_iota(I32, (t, t), 1)
            s = jnp.where(key <= qry, s, -jnp.inf)
        m_prev = m_ref[...]
        m_new = jnp.maximum(m_prev, jnp.max(s, axis=0, keepdims=True))
        alpha = jnp.exp2(m_prev - m_new)
        p = jnp.exp2(s - m_new)
        l_ref[...] = alpha * l_ref[...] + jnp.sum(p, axis=0, keepdims=True)
        acc_ref[...] = alpha * acc_ref[...] + jnp.dot(
            vt_ref[:, pl.ds(ks, t)], p.astype(BF16), preferred_element_type=F32)
        m_ref[...] = m_new

    first = js_ref[(b * n_heads + h) * nq + i]
    n_full = i - first
    scores(first, 0)

    def body(pair, carry):
        j = first + 2 * pair
        scores(j + 1, 1)
        accumulate(j, 0, False)
        scores(j + 2, 0)
        accumulate(j + 1, 1, False)
        return carry

    lax.fori_loop(0, n_full // 2, body, 0)

    @pl.when(n_full % 2 == 1)
    def _():
        scores(i, 1)
        accumulate(i - 1, 0, False)
        accumulate(i, 1, True)

    @pl.when(n_full % 2 == 0)
    def _():
        accumulate(i, 0, True)

    o_ref[...] = (acc_ref[...] / l_ref[...]).T.astype(o_ref.dtype)


def _fox_first_blocks(stats, fcum, n_heads, t):
    B, S, _ = stats.shape
    qn = jnp.sqrt(stats[..., :n_heads]) / LOG2_E
    kn = jnp.sqrt(jnp.max(stats[..., n_heads:2 * n_heads], axis=1))
    diag = stats[..., 2 * n_heads:3 * n_heads] / LOG2_E
    f = fcum[..., :n_heads]
    need = qn * kn[:, None, :] * BOUND_SLACK - diag + f + (UNDERFLOW_LOG + 1.0)
    nq = S // t
    need = jnp.max(need.reshape(B, nq, t, n_heads), axis=2)
    f_end = f.reshape(B, nq, t, n_heads)[:, :, t - 1, :]
    skippable = f_end[:, None, :, :] > need[:, :, None, :]
    first = jnp.sum(skippable.astype(I32), axis=2)
    first = jnp.minimum(first, jnp.arange(nq, dtype=I32)[None, :, None])
    return first.transpose(0, 2, 1).reshape(-1)


def _fox_attention(q, k_aug, vt, fcum, stats, n_heads):
    B, S, D = q.shape
    t = min(ATTN_TILE, S)
    nq = S // t
    first = _fox_first_blocks(stats, fcum, n_heads, t)
    grid_spec = pltpu.PrefetchScalarGridSpec(
        num_scalar_prefetch=1,
        grid=(B, n_heads, nq),
        in_specs=[pl.BlockSpec((None, t, HEAD_DIM), lambda b, h, i, js: (b, i, h)),
                  pl.BlockSpec((None, S, 2 * LANES), lambda b, h, i, js: (b, 0, h)),
                  pl.BlockSpec((None, HEAD_DIM, S), lambda b, h, i, js: (b, h, 0))],
        out_specs=pl.BlockSpec((None, t, HEAD_DIM), lambda b, h, i, js: (b, i, h)),
        scratch_shapes=[pltpu.VMEM((1, t), F32), pltpu.VMEM((1, t), F32),
                        pltpu.VMEM((HEAD_DIM, t), F32), pltpu.VMEM((2, t, t), F32)],
    )
    return pl.pallas_call(
        functools.partial(_fox_kernel, n_heads=n_heads),
        grid_spec=grid_spec,
        out_shape=jax.ShapeDtypeStruct((B, S, D), BF16),
        compiler_params=_params("parallel", "parallel", "arbitrary"),
        name="fox_attention",
    )(first, q, k_aug, vt)


def _sb_kernel(q_ref, k_ref, vt_ref, o_ref, carry_ref, acc_ref):
    t = q_ref.shape[0]
    n_local = q_ref.shape[1] // HEAD_DIM
    i = pl.program_id(2)
    carry_ref[...] = jnp.zeros_like(carry_ref)
    acc_ref[...] = jnp.zeros_like(acc_ref)
    key = lax.broadcasted_iota(I32, (t, t), 0)
    qry = lax.broadcasted_iota(I32, (t, t), 1)
    later = jnp.where(qry > key, 1.0, 0.0).astype(BF16)
    later2 = jnp.concatenate([later, later], axis=1)

    def block(j, masked):
        ks = pl.multiple_of(j * t, t)
        feats = [slice(g * HEAD_DIM, (g + 1) * HEAD_DIM) for g in range(n_local)]
        zs = [lax.dot_general(k_ref[pl.ds(ks, t), f], q_ref[:, f], NT_DIMS, preferred_element_type=F32)
              for f in feats]
        log1ms, suffixes = [], []
        for z in zs:
            log1m = _log_sigmoid(-z)
            if masked:
                log1m = jnp.where(key < qry, log1m, 0.0)
            hi = log1m.astype(BF16)
            mid = (log1m - hi.astype(F32)).astype(BF16)
            log1ms.append(log1m)
            suffixes.append(jnp.dot(later2, jnp.concatenate([hi, mid], axis=0), preferred_element_type=F32))
        for g, f in enumerate(feats):
            la = zs[g] + log1ms[g] + suffixes[g] + carry_ref[g:g + 1, :]
            if masked:
                la = jnp.where(key < qry, la, -jnp.inf)
            a = jnp.exp(la)
            acc_ref[f, :] += jnp.dot(vt_ref[f, pl.ds(ks, t)], a.astype(BF16), preferred_element_type=F32)
            carry_ref[g:g + 1, :] += jnp.sum(log1ms[g], axis=0, keepdims=True)

    block(i, True)

    def cond(state):
        j, top = state
        return jnp.logical_and(j >= 0, top > -UNDERFLOW_LOG)

    def body(state):
        j, _ = state
        block(j, False)
        return j - 1, jnp.max(carry_ref[...])

    lax.while_loop(cond, body, (i - 1, jnp.max(carry_ref[...])))
    for g in range(n_local):
        feat = slice(g * HEAD_DIM, (g + 1) * HEAD_DIM)
        o_ref[:, feat] = acc_ref[feat, :].T.astype(o_ref.dtype)


def _sb_attention(q, k, vt, n_heads):
    B, S, D = q.shape
    t = min(SB_TILE, S)
    width = SB_HEADS_PER_STEP * HEAD_DIM
    return pl.pallas_call(
        _sb_kernel,
        grid=(B, n_heads // SB_HEADS_PER_STEP, S // t),
        in_specs=[pl.BlockSpec((None, t, width), lambda b, h, i: (b, i, h)),
                  pl.BlockSpec((None, S, width), lambda b, h, i: (b, 0, h)),
                  pl.BlockSpec((None, width, S), lambda b, h, i: (b, h, 0))],
        out_specs=pl.BlockSpec((None, t, width), lambda b, h, i: (b, i, h)),
        out_shape=jax.ShapeDtypeStruct((B, S, D), BF16),
        scratch_shapes=[pltpu.VMEM((SB_HEADS_PER_STEP, t), F32), pltpu.VMEM((width, t), F32)],
        compiler_params=_params("parallel", "parallel", "arbitrary"),
        name="sb_attention",
    )(q, k, vt)


def _outproj_ln_kernel(o_ref, w_ref, x_ref, gate_ref, g_ref, b_ref, out_ref):
    y = jnp.dot(o_ref[...], w_ref[...], preferred_element_type=F32)
    z = DEEPNORM_ALPHA * x_ref[...] + gate_ref[...] * y
    out_ref[...] = _layer_norm(z, g_ref[...], b_ref[...])


def _out_projection_norm(o, w_bf16, x, x_batch, gate, ln_g, ln_b):
    B, S, D = o.shape
    tm = min(ROW_TILE, S)
    row_spec = pl.BlockSpec((None, tm, D), lambda b, i: (b, i, 0))
    return pl.pallas_call(
        _outproj_ln_kernel,
        grid=(B, S // tm),
        in_specs=[row_spec,
                  pl.BlockSpec((D, D), lambda b, i: (0, 0)),
                  pl.BlockSpec((None, tm, D), lambda b, i: (b + x_batch, i, 0)),
                  pl.BlockSpec((None, 1, D), lambda b, i: (b, 0, 0)),
                  pl.BlockSpec((1, D), lambda b, i: (0, 0)),
                  pl.BlockSpec((1, D), lambda b, i: (0, 0))],
        out_specs=row_spec,
        out_shape=jax.ShapeDtypeStruct((B, S, D), F32),
        compiler_params=_params("parallel", "parallel"),
        name="out_projection_norm",
    )(o, w_bf16, x, gate.reshape(B, 1, D), ln_g.reshape(1, D), ln_b.reshape(1, D))


def _router_kernel(x_ref, sh_ref, sc_ref, rwt_ref, rb_ref,
                   h_ref, idx_ref, wts_ref, pos_ref, cnt_ref, carry_ref):
    tm = x_ref.shape[0]
    n_experts = rwt_ref.shape[0]

    @pl.when(jnp.logical_and(pl.program_id(0) == 0, pl.program_id(1) == 0))
    def _():
        carry_ref[...] = jnp.zeros_like(carry_ref)

    h = x_ref[...] * (1.0 + sc_ref[...]) + sh_ref[...]
    h_ref[...] = _pack_halves(h)
    logits = lax.dot_general(rwt_ref[...], h, NT_DIMS, preferred_element_type=F32,
                             precision=lax.Precision.HIGHEST)
    scores = jax.nn.sigmoid(logits)
    expert = lax.broadcasted_iota(I32, (n_experts, tm), 0).astype(F32)
    slot = lax.broadcasted_iota(I32, (TOP_K, tm), 0)
    sel = scores + rb_ref[...]

    chosen = jnp.zeros((n_experts, tm), F32)
    idx_acc = jnp.zeros((TOP_K, tm), F32)
    w_acc = jnp.zeros((TOP_K, tm), F32)
    total = jnp.zeros((1, tm), F32)
    picks = []
    for k in range(TOP_K):
        best = jnp.max(sel, axis=0, keepdims=True)
        pick = jnp.min(jnp.where(sel == best, expert, float(n_experts)), axis=0, keepdims=True)
        hit = expert == pick
        score = jnp.sum(jnp.where(hit, scores, 0.0), axis=0, keepdims=True)
        sel = jnp.where(hit, -jnp.inf, sel)
        chosen = jnp.where(hit, 1.0, chosen)
        idx_acc = jnp.where(slot == k, pick, idx_acc)
        w_acc = jnp.where(slot == k, score, w_acc)
        total = total + score
        picks.append(pick)
    w_acc = w_acc / total * ROUTED_SCALE

    row = lax.broadcasted_iota(I32, (tm, tm), 0)
    col = lax.broadcasted_iota(I32, (tm, tm), 1)
    earlier = jnp.where(row < col, 1.0, 0.0).astype(BF16)
    chosen_b = chosen.astype(BF16)
    before = jnp.dot(chosen_b, earlier, preferred_element_type=F32) + carry_ref[...]
    pos_acc = jnp.zeros((TOP_K, tm), F32)
    for k in range(TOP_K):
        rank = jnp.sum(jnp.where(expert == picks[k], before, 0.0), axis=0, keepdims=True)
        pos_acc = jnp.where(slot == k, rank, pos_acc)

    carry_ref[...] += jnp.dot(chosen_b, jnp.ones((tm, tm), BF16), preferred_element_type=F32)
    idx_ref[...] = idx_acc.astype(I32)
    wts_ref[...] = w_acc
    pos_ref[...] = pos_acc.astype(I32)
    cnt_ref[...] = carry_ref[:, :LANES]


def _router(x, shift, scale, router_w, router_bias):
    B, S, D = x.shape
    E = router_w.shape[1]
    tm = min(ROW_TILE, S)
    row_spec = pl.BlockSpec((None, tm, D), lambda b, i: (b, i, 0))
    vec_spec = pl.BlockSpec((None, 1, D), lambda b, i: (b, 0, 0))
    k_spec = pl.BlockSpec((None, TOP_K, tm), lambda b, i: (b, 0, i))
    return pl.pallas_call(
        _router_kernel,
        grid=(B, S // tm),
        in_specs=[row_spec, vec_spec, vec_spec,
                  pl.BlockSpec((E, D), lambda b, i: (0, 0)),
                  pl.BlockSpec((E, tm), lambda b, i: (0, 0))],
        out_specs=[pl.BlockSpec((None, tm, D // 2), lambda b, i: (b, i, 0)), k_spec, k_spec, k_spec,
                   pl.BlockSpec((E, LANES), lambda b, i: (0, 0))],
        out_shape=[jax.ShapeDtypeStruct((B, S, D // 2), U32),
                   jax.ShapeDtypeStruct((B, TOP_K, S), I32),
                   jax.ShapeDtypeStruct((B, TOP_K, S), F32),
                   jax.ShapeDtypeStruct((B, TOP_K, S), I32),
                   jax.ShapeDtypeStruct((E, LANES), F32)],
        scratch_shapes=[pltpu.VMEM((E, tm), F32)],
        compiler_params=_params("arbitrary", "arbitrary"),
        name="router",
    )(x, shift.reshape(B, 1, D), scale.reshape(B, 1, D), router_w.T,
      jnp.broadcast_to(router_bias[:, None], (E, tm)))


def _sc_mesh():
    return plsc.VectorSubcoreMesh(core_axis_name="core", subcore_axis_name="subcore")


def _sc_dispatch(h, dest, n_rows):
    T, D = h.shape
    B, K, S = dest.shape
    W = SC_WINDOW
    per_worker = T // SC_WORKERS
    dest_w = dest.reshape(B, K, S // W, W).transpose(0, 2, 1, 3).reshape(T * K)

    @pl.kernel(out_type=jax.ShapeDtypeStruct((n_rows, D), h.dtype), mesh=_sc_mesh(),
               scratch_types=[pltpu.VMEM((K * W,), I32), pltpu.VMEM((W, D), h.dtype)])
    def dispatch(h_hbm, idx_hbm, rows_hbm, idx_v, buf_v):
        worker = lax.axis_index("core") * SC_SUBCORES + lax.axis_index("subcore")

        @pl.loop(0, per_worker // W)
        def _(s):
            base = pl.multiple_of(worker * per_worker + s * W, W)
            pltpu.sync_copy(idx_hbm.at[pl.ds(base * K, K * W)], idx_v)
            pltpu.sync_copy(h_hbm.at[pl.ds(base, W)], buf_v)
            for k in range(K):
                pltpu.sync_copy(buf_v, rows_hbm.at[idx_v.at[pl.ds(k * W, W)]])

    return dispatch(h, dest_w)


def _sc_gather(rows, index):
    R = index.shape[0]
    D = rows.shape[1]
    W = SC_WINDOW
    per_worker = R // SC_WORKERS

    @pl.kernel(out_type=jax.ShapeDtypeStruct((R, D), rows.dtype), mesh=_sc_mesh(),
               scratch_types=[pltpu.VMEM((W,), I32), pltpu.VMEM((W, D), rows.dtype)])
    def gather(rows_hbm, idx_hbm, out_hbm, idx_v, buf_v):
        worker = lax.axis_index("core") * SC_SUBCORES + lax.axis_index("subcore")

        @pl.loop(0, per_worker // W)
        def _(s):
            base = pl.multiple_of(worker * per_worker + s * W, W)
            pltpu.sync_copy(idx_hbm.at[pl.ds(base, W)], idx_v)
            pltpu.sync_copy(rows_hbm.at[idx_v], buf_v)
            pltpu.sync_copy(buf_v, out_hbm.at[pl.ds(base, W)])

    return gather(rows, index)


def _swiglu_packed(x_packed, w_gate, w_up, w_down):
    half = x_packed.shape[1]
    lo, hi = _unpack_halves(x_packed)
    wg = w_gate.astype(BF16)
    wu = w_up.astype(BF16)
    g = (jnp.dot(lo, wg[:half, :], preferred_element_type=F32)
         + jnp.dot(hi, wg[half:, :], preferred_element_type=F32))
    u = (jnp.dot(lo, wu[:half, :], preferred_element_type=F32)
         + jnp.dot(hi, wu[half:, :], preferred_element_type=F32))
    a = (g * jax.nn.sigmoid(g) * u).astype(BF16)
    return jnp.dot(a, w_down.astype(BF16), preferred_element_type=F32)


def _expert_kernel(be_ref, used_ref, x_ref, wg_ref, wu_ref, wd_ref, o_ref):
    @pl.when(pl.program_id(0) < used_ref[0])
    def _():
        o_ref[...] = _pack_halves(_swiglu_packed(x_ref[...], wg_ref[...], wu_ref[...], wd_ref[...]))


def _experts(rows, block_expert, n_used, layer, w_gate, w_up, w_down):
    R, half = rows.shape
    D = 2 * half
    F = w_gate.shape[3]
    blk = EXPERT_BLOCK
    grid_spec = pltpu.PrefetchScalarGridSpec(
        num_scalar_prefetch=2,
        grid=(R // blk,),
        in_specs=[pl.BlockSpec((blk, half), lambda r, be, nu: (jnp.minimum(r, nu[0] - 1), 0)),
                  pl.BlockSpec((None, None, D, F), lambda r, be, nu: (layer, be[r], 0, 0)),
                  pl.BlockSpec((None, None, D, F), lambda r, be, nu: (layer, be[r], 0, 0)),
                  pl.BlockSpec((None, None, F, D), lambda r, be, nu: (layer, be[r], 0, 0))],
        out_specs=pl.BlockSpec((blk, half), lambda r, be, nu: (jnp.minimum(r, nu[0] - 1), 0)),
    )
    return pl.pallas_call(
        _expert_kernel,
        grid_spec=grid_spec,
        out_shape=jax.ShapeDtypeStruct((R, half), U32),
        compiler_params=_params("arbitrary"),
        name="routed_experts",
    )(block_expert, n_used, rows, w_gate, w_up, w_down)


def _moe_out_kernel(ga_ref, w_ref, h_ref, sg_ref, su_ref, sd_ref, x_ref, gate_ref, g_ref, b_ref, out_ref):
    half = h_ref.shape[1]
    y = _swiglu_packed(h_ref[...], sg_ref[...], su_ref[...], sd_ref[...])
    w = w_ref[...]
    y_lo = y[:, :half]
    y_hi = y[:, half:]
    for k in range(TOP_K):
        packed = ga_ref[k]
        wk = w[:, k:k + 1]
        y_lo = y_lo + wk * lax.bitcast_convert_type(packed << 16, F32)
        y_hi = y_hi + wk * lax.bitcast_convert_type(packed & jnp.uint32(0xFFFF0000), F32)
    y = jnp.concatenate([y_lo, y_hi], axis=1)
    z = DEEPNORM_ALPHA * x_ref[...] + gate_ref[...] * y
    out_ref[...] = _layer_norm(z, g_ref[...], b_ref[...])


def _moe_out_into_kernel(prev_ref, *refs):
    del prev_ref
    _moe_out_kernel(*refs)


def _moe_output_norm(gathered, wts, h, layer, s_gate, s_up, s_down, x, gate, ln_g, ln_b, out_batch, out_total, prev):
    B, S, D = x.shape
    half = D // 2
    F = s_gate.shape[2]
    tm = min(ROW_TILE // 2, S)
    row_spec = pl.BlockSpec((None, tm, D), lambda b, i: (b, i, 0))
    carried = () if prev is None else (prev,)
    return pl.pallas_call(
        _moe_out_kernel if prev is None else _moe_out_into_kernel,
        grid=(B, S // tm),
        in_specs=[pl.BlockSpec(memory_space=pl.ANY)] * len(carried)
        + [pl.BlockSpec((TOP_K, None, tm, half), lambda b, i: (0, b, i, 0)),
                  pl.BlockSpec((None, tm, TOP_K), lambda b, i: (b, i, 0)),
                  pl.BlockSpec((None, tm, half), lambda b, i: (b, i, 0)),
                  pl.BlockSpec((None, D, F), lambda b, i: (layer, 0, 0)),
                  pl.BlockSpec((None, D, F), lambda b, i: (layer, 0, 0)),
                  pl.BlockSpec((None, F, D), lambda b, i: (layer, 0, 0)),
                  row_spec,
                  pl.BlockSpec((None, 1, D), lambda b, i: (b, 0, 0)),
                  pl.BlockSpec((1, D), lambda b, i: (0, 0)),
                  pl.BlockSpec((1, D), lambda b, i: (0, 0))],
        out_specs=pl.BlockSpec((None, tm, D), lambda b, i: (b + out_batch, i, 0)),
        out_shape=jax.ShapeDtypeStruct((out_total, S, D), F32),
        input_output_aliases={} if prev is None else {0: 0},
        compiler_params=_params("parallel", "parallel"),
        name="moe_output_norm",
    )(*carried, gathered.reshape(TOP_K, B, S, half), wts, h, s_gate, s_up, s_down, x,
      gate.reshape(B, 1, D), ln_g.reshape(1, D), ln_b.reshape(1, D))


def _moe_layer(x, shift, scale, gate, router_w, router_bias, layer, w_gate, w_up, w_down,
               s_gate, s_up, s_down, ln_g, ln_b, out_batch=0, out_total=None, prev=None):
    B, S, D = x.shape
    T = B * S
    E = router_w.shape[1]
    blk = EXPERT_BLOCK
    h, idx, wts, pos, cnt = _router(x, shift, scale, router_w, router_bias)

    counts = cnt[:, 0].astype(I32)
    padded = (counts + blk - 1) // blk * blk
    pend = jnp.cumsum(padded)
    pstart = pend - padded
    n_blocks = -(-T * TOP_K // blk) + E
    onehot = idx[..., None] == jnp.arange(E, dtype=I32)
    dest = jnp.sum(jnp.where(onehot, pstart, 0), axis=-1) + pos
    block_expert = jnp.minimum(
        jnp.sum((jnp.arange(n_blocks, dtype=I32)[:, None] * blk >= pend[None, :]).astype(I32), axis=1),
        E - 1).astype(I32)
    n_used = (pend[-1:] // blk).astype(I32)

    rows = _sc_dispatch(h.reshape(T, D // 2), dest, n_blocks * blk)
    out_rows = _experts(rows, block_expert, n_used, layer, w_gate, w_up, w_down)
    gathered = _sc_gather(out_rows, dest.transpose(1, 0, 2).reshape(TOP_K * T))
    return _moe_output_norm(gathered, wts.transpose(0, 2, 1), h, layer, s_gate, s_up, s_down, x, gate, ln_g, ln_b,
                            out_batch, B if out_total is None else out_total, prev)


def kernel(x, c, fox_w_in, fox_b_f, fox_w_o, sb_w_in, sb_w_o, ada_w, ada_b, ln_attn_g, ln_attn_b,
           router_w, router_bias, exp_w_gate, exp_w_up, exp_w_down,
           shared_w_gate, shared_w_up, shared_w_down, ln_ffn_g, ln_ffn_b):
    B, S, D = x.shape
    H = D // HEAD_DIM
    depth = ada_w.shape[0]
    mod = _ada_mod(c, ada_w, ada_b)
    xs = [None] * B
    out = None
    for i in range(depth):
        j = i // 2
        w_in = fox_w_in[j] if i % 2 == 0 else sb_w_in[j]
        wq = (w_in[:, :D] * (ATTN_SCALE * LOG2_E if i % 2 == 0 else ATTN_SCALE)).astype(BF16)
        wk = w_in[:, D:2 * D].astype(BF16)
        wvt = w_in[:, 2 * D:3 * D].T.astype(BF16)
        if i % 2 == 0:
            wf = jnp.zeros((D, LANES), F32).at[:, :H].set(w_in[:, 3 * D:]).astype(BF16)
            bf = jnp.zeros((1, LANES), F32).at[0, :H].set(fox_b_f[j])
            w_o = fox_w_o[j].astype(BF16)
        else:
            w_o = sb_w_o[j].astype(BF16)
        for b in range(B):
            xb, x_batch = (x, b) if i == 0 else (xs[b], 0)
            sh_a, sc_a, g_a, sh_m, sc_m, g_m = jnp.split(mod[i, b:b + 1], 6, axis=-1)
            if i % 2 == 0:
                q, k_aug, vt, fcum, stats = _in_projection_forget(xb, x_batch, sh_a, sc_a, wq, wk, wvt, wf, bf)
                o = _fox_attention(q, k_aug, vt, fcum, stats, H)
            else:
                q, k, vt = _in_projection(xb, sh_a, sc_a, wq, wk, wvt)
                o = _sb_attention(q, k, vt, H)
            xb = _out_projection_norm(o, w_o, xb, x_batch, 1.0 + g_a, ln_attn_g[i], ln_attn_b[i])
            moe_args = (xb, sh_m, sc_m, 1.0 + g_m, router_w[i], router_bias[i], i, exp_w_gate, exp_w_up, exp_w_down,
                        shared_w_gate, shared_w_up, shared_w_down, ln_ffn_g[i], ln_ffn_b[i])
            if i + 1 < depth:
                xs[b] = _moe_layer(*moe_args)
            else:
                out = _moe_layer(*moe_args, out_batch=b, out_total=B, prev=out)
    return out
```

```python
import functools
import math

import jax
import jax.numpy as jnp
from jax import lax
from jax.experimental import pallas as pl
from jax.experimental.pallas import tpu as pltpu
from jax.experimental.pallas import tpu_sc as plsc

F32 = jnp.float32
BF16 = jnp.bfloat16
I32 = jnp.int32
U32 = jnp.uint32

HEAD_DIM = 128
TOP_K = 8
ROUTED_SCALE = 2.5
LN_EPS = 1e-5
DEPTH = 2
DEEPNORM_ALPHA = (2 * DEPTH) ** 0.25
ATTN_SCALE = 1.0 / math.sqrt(HEAD_DIM)
LOG2_E = math.log2(math.e)

LANES = 128
UNDERFLOW_LOG = 105.0
VMEM_LIMIT = 56 * 1024 * 1024

ROW_TILE = 512
ATTN_TILE = 512
SB_TILE = 256
SB_HEADS_PER_STEP = 2
EXPERT_BLOCK = 512
SC_WINDOW = 128
SC_SUBCORES = 16
SC_WORKERS = 2 * SC_SUBCORES
N_BIAS_COLS = 3
BOUND_SLACK = 1.008

NT_DIMS = (((1,), (1,)), ((), ()))


def _params(*sem):
    return pltpu.CompilerParams(dimension_semantics=sem, vmem_limit_bytes=VMEM_LIMIT)


def _log_sigmoid(z):
    return jnp.minimum(z, 0.0) - jnp.log1p(jnp.exp(-jnp.abs(z)))


def _split3(x):
    hi = x.astype(BF16)
    r = x - hi.astype(F32)
    mid = r.astype(BF16)
    lo = (r - mid.astype(F32)).astype(BF16)
    return hi, mid, lo


def _pack_halves(x):
    n = x.shape[1] // 2
    bits = lax.bitcast_convert_type(x.astype(BF16).astype(F32), U32)
    return (bits[:, :n] >> 16) | (bits[:, n:] & jnp.uint32(0xFFFF0000))


def _unpack_halves(w):
    lo = lax.bitcast_convert_type(w << 16, F32).astype(BF16)
    hi = lax.bitcast_convert_type(w & jnp.uint32(0xFFFF0000), F32).astype(BF16)
    return lo, hi


def _layer_norm(z, g, b):
    mu = jnp.mean(z, axis=-1, keepdims=True)
    d = z - mu
    var = jnp.mean(d * d, axis=-1, keepdims=True)
    return d * lax.rsqrt(var + LN_EPS) * g + b


def _ada_kernel(c_ref, w_ref, b_ref, o_ref):
    c = c_ref[...]
    act = c * jax.nn.sigmoid(c)
    o_ref[...] = jnp.dot(act, w_ref[...], preferred_element_type=F32,
                         precision=lax.Precision.HIGHEST) + b_ref[...]


def _ada_mod(c, ada_w, ada_b):
    B, D = c.shape
    L, _, N = ada_w.shape
    cp = jnp.zeros((8, D), F32).at[:B].set(c)
    out = pl.pallas_call(
        _ada_kernel,
        grid=(L, N // D),
        in_specs=[pl.BlockSpec((8, D), lambda l, j: (0, 0)),
                  pl.BlockSpec((None, D, D), lambda l, j: (l, 0, j)),
                  pl.BlockSpec((None, 1, D), lambda l, j: (l, 0, j))],
        out_specs=pl.BlockSpec((None, 8, D), lambda l, j: (l, 0, j)),
        out_shape=jax.ShapeDtypeStruct((L, 8, N), F32),
        compiler_params=_params("parallel", "parallel"),
        name="ada_mod",
    )(cp, ada_w, ada_b.reshape(L, 1, N))
    return out[:, :B]


def _project_qkv(h, wq_ref, wk_ref, wvt_ref, q_ref, k_ref, vt_ref):
    q_ref[...] = jnp.dot(h, wq_ref[...], preferred_element_type=F32).astype(BF16)
    k_ref[...] = jnp.dot(h, wk_ref[...], preferred_element_type=F32).astype(BF16)
    vt_ref[...] = lax.dot_general(wvt_ref[...], h, NT_DIMS, preferred_element_type=F32).astype(BF16)


def _inproj_kernel(x_ref, sh_ref, sc_ref, wq_ref, wk_ref, wvt_ref, q_ref, k_ref, vt_ref):
    h = (x_ref[...] * (1.0 + sc_ref[...]) + sh_ref[...]).astype(BF16)
    _project_qkv(h, wq_ref, wk_ref, wvt_ref, q_ref, k_ref, vt_ref)


def _inproj_forget_kernel(x_ref, sh_ref, sc_ref, wq_ref, wk_ref, wvt_ref, wf_ref, bf_ref, sel_ref, hsum_ref,
                          q_ref, ka_ref, vt_ref, f_ref, stats_ref, carry_ref):
    tm, D = x_ref.shape
    n_heads = D // HEAD_DIM
    h = (x_ref[...] * (1.0 + sc_ref[...]) + sh_ref[...]).astype(BF16)
    q = jnp.dot(h, wq_ref[...], preferred_element_type=F32).astype(BF16)
    k = jnp.dot(h, wk_ref[...], preferred_element_type=F32).astype(BF16)
    q_ref[...] = q
    vt_ref[...] = lax.dot_general(wvt_ref[...], h, NT_DIMS, preferred_element_type=F32).astype(BF16)

    @pl.when(pl.program_id(1) == 0)
    def _():
        carry_ref[...] = jnp.zeros_like(carry_ref)

    logf = _log_sigmoid(jnp.dot(h, wf_ref[...], preferred_element_type=F32) + bf_ref[...])
    row = lax.broadcasted_iota(I32, (tm, tm), 0)
    col = lax.broadcasted_iota(I32, (tm, tm), 1)
    tri = jnp.where(row >= col, 1.0, 0.0).astype(BF16)
    hi, mid, lo = _split3(logf)
    cs = (jnp.dot(tri, hi, preferred_element_type=F32)
          + jnp.dot(tri, mid, preferred_element_type=F32)
          + jnp.dot(tri, lo, preferred_element_type=F32))
    cum = cs + carry_ref[...]
    f_ref[...] = cum
    carry_ref[...] = cum[tm - 1:tm, :]

    pieces = jnp.concatenate(_split3(-LOG2_E * cum), axis=1)
    for g in range(n_heads):
        ka_ref[:, 2 * g * LANES:(2 * g + 1) * LANES] = k[:, g * HEAD_DIM:(g + 1) * HEAD_DIM]
    for pair in range(n_heads // 2):
        moved = jnp.dot(pieces, sel_ref[pair], preferred_element_type=F32).astype(BF16)
        for half in range(2):
            g = 2 * pair + half
            ka_ref[:, (2 * g + 1) * LANES:(2 * g + 2) * LANES] = moved[:, half * LANES:(half + 1) * LANES]

    qf = q.astype(F32)
    kf = k.astype(F32)
    stats_ref[...] = (jnp.dot((qf * qf).astype(BF16), hsum_ref[0], preferred_element_type=F32)
                      + jnp.dot((kf * kf).astype(BF16), hsum_ref[1], preferred_element_type=F32)
                      + jnp.dot((qf * kf).astype(BF16), hsum_ref[2], preferred_element_type=F32))


def _in_projection(x, shift, scale, wq, wk, wvt):
    B, S, D = x.shape
    tm = min(ROW_TILE, S)
    row_spec = pl.BlockSpec((None, tm, D), lambda b, i: (b, i, 0))
    vec_spec = pl.BlockSpec((None, 1, D), lambda b, i: (b, 0, 0))
    w_spec = pl.BlockSpec((D, D), lambda b, i: (0, 0))
    return pl.pallas_call(
        _inproj_kernel,
        grid=(B, S // tm),
        in_specs=[row_spec, vec_spec, vec_spec, w_spec, w_spec, w_spec],
        out_specs=[row_spec, row_spec, pl.BlockSpec((None, D, tm), lambda b, i: (b, 0, i))],
        out_shape=[jax.ShapeDtypeStruct((B, S, D), BF16), jax.ShapeDtypeStruct((B, S, D), BF16),
                   jax.ShapeDtypeStruct((B, D, S), BF16)],
        compiler_params=_params("parallel", "parallel"),
        name="in_projection",
    )(x, shift.reshape(B, 1, D), scale.reshape(B, 1, D), wq, wk, wvt)


def _in_projection_forget(x, x_batch, shift, scale, wq, wk, wvt, wf, bf):
    _, S, D = x.shape
    B = shift.shape[0]
    n_heads = D // HEAD_DIM
    tm = min(ROW_TILE, S)
    x_spec = pl.BlockSpec((None, tm, D), lambda b, i: (b + x_batch, i, 0))
    row_spec = pl.BlockSpec((None, tm, D), lambda b, i: (b, i, 0))
    vec_spec = pl.BlockSpec((None, 1, D), lambda b, i: (b, 0, 0))
    w_spec = pl.BlockSpec((D, D), lambda b, i: (0, 0))
    lane_spec = pl.BlockSpec((None, tm, LANES), lambda b, i: (b, i, 0))
    heads = jnp.arange(n_heads)
    sel = jnp.zeros((n_heads // 2, N_BIAS_COLS * LANES, 2 * LANES), BF16)
    for p in range(N_BIAS_COLS):
        sel = sel.at[heads // 2, p * LANES + heads, (heads % 2) * LANES + p].set(1.0)
    feat = jnp.arange(D)
    hsum = jnp.zeros((3, D, LANES), BF16)
    for c in range(3):
        hsum = hsum.at[c, feat, c * n_heads + feat // HEAD_DIM].set(1.0)
    return pl.pallas_call(
        _inproj_forget_kernel,
        grid=(B, S // tm),
        in_specs=[x_spec, vec_spec, vec_spec, w_spec, w_spec, w_spec,
                  pl.BlockSpec((D, LANES), lambda b, i: (0, 0)),
                  pl.BlockSpec((1, LANES), lambda b, i: (0, 0)),
                  pl.BlockSpec((n_heads // 2, N_BIAS_COLS * LANES, 2 * LANES), lambda b, i: (0, 0, 0)),
                  pl.BlockSpec((3, D, LANES), lambda b, i: (0, 0, 0))],
        out_specs=[row_spec, pl.BlockSpec((None, tm, 2 * D), lambda b, i: (b, i, 0)),
                   pl.BlockSpec((None, D, tm), lambda b, i: (b, 0, i)), lane_spec, lane_spec],
        out_shape=[jax.ShapeDtypeStruct((B, S, D), BF16), jax.ShapeDtypeStruct((B, S, 2 * D), BF16),
                   jax.ShapeDtypeStruct((B, D, S), BF16), jax.ShapeDtypeStruct((B, S, LANES), F32),
                   jax.ShapeDtypeStruct((B, S, LANES), F32)],
        scratch_shapes=[pltpu.VMEM((1, LANES), F32)],
        compiler_params=_params("parallel", "arbitrary"),
        name="in_projection_forget",
    )(x, shift.reshape(B, 1, D), scale.reshape(B, 1, D), wq, wk, wvt, wf, bf, sel, hsum)


def _fox_kernel(js_ref, q_ref, ka_ref, vt_ref, o_ref, m_ref, l_ref, acc_ref, s_ref, *, n_heads):
    t = q_ref.shape[0]
    b, h, i = pl.program_id(0), pl.program_id(1), pl.program_id(2)
    nq = pl.num_programs(2)
    m_ref[...] = jnp.full(m_ref.shape, -jnp.inf, F32)
    l_ref[...] = jnp.zeros_like(l_ref)
    acc_ref[...] = jnp.zeros_like(acc_ref)
    sub = lax.broadcasted_iota(I32, (LANES, t), 0)
    q_aug_t = jnp.concatenate([q_ref[...].astype(F32).T.astype(BF16),
                               jnp.where(sub < N_BIAS_COLS, 1.0, 0.0).astype(BF16)], axis=0)

    def scores(j, slot):
        ks = pl.multiple_of(j * t, t)
        s_ref[slot] = jnp.dot(ka_ref[pl.ds(ks, t), :], q_aug_t, preferred_element_type=F32)

    def accumulate(j, slot, masked):
        ks = pl.multiple_of(j * t, t)
        s = s_ref[slot]
        if masked:
            key = lax.broadcasted_iota(I32, (t, t), 0)
            qry = lax.broadcasted_iota(I32, (t, t), 1)
            s = jnp.where(key <= qry, s, -jnp.inf)
        m_prev = m_ref[...]
        m_new = jnp.maximum(m_prev, jnp.max(s, axis=0, keepdims=True))
        alpha = jnp.exp2(m_prev - m_new)
        p = jnp.exp2(s - m_new)
        l_ref[...] = alpha * l_ref[...] + jnp.sum(p, axis=0, keepdims=True)
        acc_ref[...] = alpha * acc_ref[...] + jnp.dot(
            vt_ref[:, pl.ds(ks, t)], p.astype(BF16), preferred_element_type=F32)
        m_ref[...] = m_new

    first = js_ref[(b * n_heads + h) * nq + i]
    n_full = i - first
    scores(first, 0)

    def body(pair, carry):
        j = first + 2 * pair
        scores(j + 1, 1)
        accumulate(j, 0, False)
        scores(j + 2, 0)
        accumulate(j + 1, 1, False)
        return carry

    lax.fori_loop(0, n_full // 2, body, 0)

    @pl.when(n_full % 2 == 1)
    def _():
        scores(i, 1)
        accumulate(i - 1, 0, False)
        accumulate(i, 1, True)

    @pl.when(n_full % 2 == 0)
    def _():
        accumulate(i, 0, True)

    o_ref[...] = (acc_ref[...] / l_ref[...]).T.astype(o_ref.dtype)


def _fox_first_blocks(stats, fcum, n_heads, t):
    B, S, _ = stats.shape
    qn = jnp.sqrt(stats[..., :n_heads]) / LOG2_E
    kn = jnp.sqrt(jnp.max(stats[..., n_heads:2 * n_heads], axis=1))
    diag = stats[..., 2 * n_heads:3 * n_heads] / LOG2_E
    f = fcum[..., :n_heads]
    need = qn * kn[:, None, :] * BOUND_SLACK - diag + f + (UNDERFLOW_LOG + 1.0)
    nq = S // t
    need = jnp.max(need.reshape(B, nq, t, n_heads), axis=2)
    f_end = f.reshape(B, nq, t, n_heads)[:, :, t - 1, :]
    skippable = f_end[:, None, :, :] > need[:, :, None, :]
    first = jnp.sum(skippable.astype(I32), axis=2)
    first = jnp.minimum(first, jnp.arange(nq, dtype=I32)[None, :, None])
    return first.transpose(0, 2, 1).reshape(-1)


def _fox_attention(q, k_aug, vt, fcum, stats, n_heads):
    B, S, D = q.shape
    t = min(ATTN_TILE, S)
    nq = S // t
    first = _fox_first_blocks(stats, fcum, n_heads, t)
    grid_spec = pltpu.PrefetchScalarGridSpec(
        num_scalar_prefetch=1,
        grid=(B, n_heads, nq),
        in_specs=[pl.BlockSpec((None, t, HEAD_DIM), lambda b, h, i, js: (b, i, h)),
                  pl.BlockSpec((None, S, 2 * LANES), lambda b, h, i, js: (b, 0, h)),
                  pl.BlockSpec((None, HEAD_DIM, S), lambda b, h, i, js: (b, h, 0))],
        out_specs=pl.BlockSpec((None, t, HEAD_DIM), lambda b, h, i, js: (b, i, h)),
        scratch_shapes=[pltpu.VMEM((1, t), F32), pltpu.VMEM((1, t), F32),
                        pltpu.VMEM((HEAD_DIM, t), F32), pltpu.VMEM((2, t, t), F32)],
    )
    return pl.pallas_call(
        functools.partial(_fox_kernel, n_heads=n_heads),
        grid_spec=grid_spec,
        out_shape=jax.ShapeDtypeStruct((B, S, D), BF16),
        compiler_params=_params("parallel", "parallel", "arbitrary"),
        name="fox_attention",
    )(first, q, k_aug, vt)


def _sb_kernel(q_ref, k_ref, vt_ref, o_ref, carry_ref, acc_ref):
    t = q_ref.shape[0]
    n_local = q_ref.shape[1] // HEAD_DIM
    i = pl.program_id(2)
    carry_ref[...] = jnp.zeros_like(carry_ref)
    acc_ref[...] = jnp.zeros_like(acc_ref)
    key = lax.broadcasted_iota(I32, (t, t), 0)
    qry = lax.broadcasted_iota(I32, (t, t), 1)
    later = jnp.where(qry > key, 1.0, 0.0).astype(BF16)
    later2 = jnp.concatenate([later, later], axis=1)

    def block(j, masked):
        ks = pl.multiple_of(j * t, t)
        feats = [slice(g * HEAD_DIM, (g + 1) * HEAD_DIM) for g in range(n_local)]
        zs = [lax.dot_general(k_ref[pl.ds(ks, t), f], q_ref[:, f], NT_DIMS, preferred_element_type=F32)
              for f in feats]
        log1ms, suffixes = [], []
        for z in zs:
            log1m = _log_sigmoid(-z)
            if masked:
                log1m = jnp.where(key < qry, log1m, 0.0)
            hi = log1m.astype(BF16)
            mid = (log1m - hi.astype(F32)).astype(BF16)
            log1ms.append(log1m)
            suffixes.append(jnp.dot(later2, jnp.concatenate([hi, mid], axis=0), preferred_element_type=F32))
        for g, f in enumerate(feats):
            la = zs[g] + log1ms[g] + suffixes[g] + carry_ref[g:g + 1, :]
            if masked:
                la = jnp.where(key < qry, la, -jnp.inf)
            a = jnp.exp(la)
            acc_ref[f, :] += jnp.dot(vt_ref[f, pl.ds(ks, t)], a.astype(BF16), preferred_element_type=F32)
            carry_ref[g:g + 1, :] += jnp.sum(log1ms[g], axis=0, keepdims=True)

    block(i, True)

    def cond(state):
        j, top = state
        return jnp.logical_and(j >= 0, top > -UNDERFLOW_LOG)

    def body(state):
        j, _ = state
        block(j, False)
        return j - 1, jnp.max(carry_ref[...])

    lax.while_loop(cond, body, (i - 1, jnp.max(carry_ref[...])))
    for g in range(n_local):
        feat = slice(g * HEAD_DIM, (g + 1) * HEAD_DIM)
        o_ref[:, feat] = acc_ref[feat, :].T.astype(o_ref.dtype)


def _sb_attention(q, k, vt, n_heads):
    B, S, D = q.shape
    t = min(SB_TILE, S)
    width = SB_HEADS_PER_STEP * HEAD_DIM
    return pl.pallas_call(
        _sb_kernel,
        grid=(B, n_heads // SB_HEADS_PER_STEP, S // t),
        in_specs=[pl.BlockSpec((None, t, width), lambda b, h, i: (b, i, h)),
                  pl.BlockSpec((None, S, width), lambda b, h, i: (b, 0, h)),
                  pl.BlockSpec((None, width, S), lambda b, h, i: (b, h, 0))],
        out_specs=pl.BlockSpec((None, t, width), lambda b, h, i: (b, i, h)),
        out_shape=jax.ShapeDtypeStruct((B, S, D), BF16),
        scratch_shapes=[pltpu.VMEM((SB_HEADS_PER_STEP, t), F32), pltpu.VMEM((width, t), F32)],
        compiler_params=_params("parallel", "parallel", "arbitrary"),
        name="sb_attention",
    )(q, k, vt)


def _outproj_ln_kernel(o_ref, w_ref, x_ref, gate_ref, g_ref, b_ref, out_ref):
    y = jnp.dot(o_ref[...], w_ref[...], preferred_element_type=F32)
    z = DEEPNORM_ALPHA * x_ref[...] + gate_ref[...] * y
    out_ref[...] = _layer_norm(z, g_ref[...], b_ref[...])


def _out_projection_norm(o, w_bf16, x, x_batch, gate, ln_g, ln_b):
    B, S, D = o.shape
    tm = min(ROW_TILE, S)
    row_spec = pl.BlockSpec((None, tm, D), lambda b, i: (b, i, 0))
    return pl.pallas_call(
        _outproj_ln_kernel,
        grid=(B, S // tm),
        in_specs=[row_spec,
                  pl.BlockSpec((D, D), lambda b, i: (0, 0)),
                  pl.BlockSpec((None, tm, D), lambda b, i: (b + x_batch, i, 0)),
                  pl.BlockSpec((None, 1, D), lambda b, i: (b, 0, 0)),
                  pl.BlockSpec((1, D), lambda b, i: (0, 0)),
                  pl.BlockSpec((1, D), lambda b, i: (0, 0))],
        out_specs=row_spec,
        out_shape=jax.ShapeDtypeStruct((B, S, D), F32),
        compiler_params=_params("parallel", "parallel"),
        name="out_projection_norm",
    )(o, w_bf16, x, gate.reshape(B, 1, D), ln_g.reshape(1, D), ln_b.reshape(1, D))


def _router_kernel(x_ref, sh_ref, sc_ref, rwt_ref, rb_ref,
                   h_ref, idx_ref, wts_ref, pos_ref, cnt_ref, carry_ref):
    tm = x_ref.shape[0]
    n_experts = rwt_ref.shape[0]

    @pl.when(jnp.logical_and(pl.program_id(0) == 0, pl.program_id(1) == 0))
    def _():
        carry_ref[...] = jnp.zeros_like(carry_ref)

    h = x_ref[...] * (1.0 + sc_ref[...]) + sh_ref[...]
    h_ref[...] = _pack_halves(h)
    logits = lax.dot_general(rwt_ref[...], h, NT_DIMS, preferred_element_type=F32,
                             precision=lax.Precision.HIGHEST)
    scores = jax.nn.sigmoid(logits)
    expert = lax.broadcasted_iota(I32, (n_experts, tm), 0).astype(F32)
    slot = lax.broadcasted_iota(I32, (TOP_K, tm), 0)
    sel = scores + rb_ref[...]

    chosen = jnp.zeros((n_experts, tm), F32)
    idx_acc = jnp.zeros((TOP_K, tm), F32)
    w_acc = jnp.zeros((TOP_K, tm), F32)
    total = jnp.zeros((1, tm), F32)
    picks = []
    for k in range(TOP_K):
        best = jnp.max(sel, axis=0, keepdims=True)
        pick = jnp.min(jnp.where(sel == best, expert, float(n_experts)), axis=0, keepdims=True)
        hit = expert == pick
        score = jnp.sum(jnp.where(hit, scores, 0.0), axis=0, keepdims=True)
        sel = jnp.where(hit, -jnp.inf, sel)
        chosen = jnp.where(hit, 1.0, chosen)
        idx_acc = jnp.where(slot == k, pick, idx_acc)
        w_acc = jnp.where(slot == k, score, w_acc)
        total = total + score
        picks.append(pick)
    w_acc = w_acc / total * ROUTED_SCALE

    row = lax.broadcasted_iota(I32, (tm, tm), 0)
    col = lax.broadcasted_iota(I32, (tm, tm), 1)
    earlier = jnp.where(row < col, 1.0, 0.0).astype(BF16)
    chosen_b = chosen.astype(BF16)
    before = jnp.dot(chosen_b, earlier, preferred_element_type=F32) + carry_ref[...]
    pos_acc = jnp.zeros((TOP_K, tm), F32)
    for k in range(TOP_K):
        rank = jnp.sum(jnp.where(expert == picks[k], before, 0.0), axis=0, keepdims=True)
        pos_acc = jnp.where(slot == k, rank, pos_acc)

    carry_ref[...] += jnp.dot(chosen_b, jnp.ones((tm, tm), BF16), preferred_element_type=F32)
    idx_ref[...] = idx_acc.astype(I32)
    wts_ref[...] = w_acc
    pos_ref[...] = pos_acc.astype(I32)
    cnt_ref[...] = carry_ref[:, :LANES]


def _router(x, shift, scale, router_w, router_bias):
    B, S, D = x.shape
    E = router_w.shape[1]
    tm = min(ROW_TILE, S)
    row_spec = pl.BlockSpec((None, tm, D), lambda b, i: (b, i, 0))
    vec_spec = pl.BlockSpec((None, 1, D), lambda b, i: (b, 0, 0))
    k_spec = pl.BlockSpec((None, TOP_K, tm), lambda b, i: (b, 0, i))
    return pl.pallas_call(
        _router_kernel,
        grid=(B, S // tm),
        in_specs=[row_spec, vec_spec, vec_spec,
                  pl.BlockSpec((E, D), lambda b, i: (0, 0)),
                  pl.BlockSpec((E, tm), lambda b, i: (0, 0))],
        out_specs=[pl.BlockSpec((None, tm, D // 2), lambda b, i: (b, i, 0)), k_spec, k_spec, k_spec,
                   pl.BlockSpec((E, LANES), lambda b, i: (0, 0))],
        out_shape=[jax.ShapeDtypeStruct((B, S, D // 2), U32),
                   jax.ShapeDtypeStruct((B, TOP_K, S), I32),
                   jax.ShapeDtypeStruct((B, TOP_K, S), F32),
                   jax.ShapeDtypeStruct((B, TOP_K, S), I32),
                   jax.ShapeDtypeStruct((E, LANES), F32)],
        scratch_shapes=[pltpu.VMEM((E, tm), F32)],
        compiler_params=_params("arbitrary", "arbitrary"),
        name="router",
    )(x, shift.reshape(B, 1, D), scale.reshape(B, 1, D), router_w.T,
      jnp.broadcast_to(router_bias[:, None], (E, tm)))


def _sc_mesh():
    return plsc.VectorSubcoreMesh(core_axis_name="core", subcore_axis_name="subcore")


def _sc_dispatch(h, dest, n_rows):
    T, D = h.shape
    B, K, S = dest.shape
    W = SC_WINDOW
    per_worker = T // SC_WORKERS
    dest_w = dest.reshape(B, K, S // W, W).transpose(0, 2, 1, 3).reshape(T * K)

    @pl.kernel(out_type=jax.ShapeDtypeStruct((n_rows, D), h.dtype), mesh=_sc_mesh(),
               scratch_types=[pltpu.VMEM((K * W,), I32), pltpu.VMEM((W, D), h.dtype)])
    def dispatch(h_hbm, idx_hbm, rows_hbm, idx_v, buf_v):
        worker = lax.axis_index("core") * SC_SUBCORES + lax.axis_index("subcore")

        @pl.loop(0, per_worker // W)
        def _(s):
            base = pl.multiple_of(worker * per_worker + s * W, W)
            pltpu.sync_copy(idx_hbm.at[pl.ds(base * K, K * W)], idx_v)
            pltpu.sync_copy(h_hbm.at[pl.ds(base, W)], buf_v)
            for k in range(K):
                pltpu.sync_copy(buf_v, rows_hbm.at[idx_v.at[pl.ds(k * W, W)]])

    return dispatch(h, dest_w)


def _sc_gather(rows, index):
    R = index.shape[0]
    D = rows.shape[1]
    W = SC_WINDOW
    per_worker = R // SC_WORKERS

    @pl.kernel(out_type=jax.ShapeDtypeStruct((R, D), rows.dtype), mesh=_sc_mesh(),
               scratch_types=[pltpu.VMEM((W,), I32), pltpu.VMEM((W, D), rows.dtype)])
    def gather(rows_hbm, idx_hbm, out_hbm, idx_v, buf_v):
        worker = lax.axis_index("core") * SC_SUBCORES + lax.axis_index("subcore")

        @pl.loop(0, per_worker // W)
        def _(s):
            base = pl.multiple_of(worker * per_worker + s * W, W)
            pltpu.sync_copy(idx_hbm.at[pl.ds(base, W)], idx_v)
            pltpu.sync_copy(rows_hbm.at[idx_v], buf_v)
            pltpu.sync_copy(buf_v, out_hbm.at[pl.ds(base, W)])

    return gather(rows, index)


def _swiglu_packed(x_packed, w_gate, w_up, w_down):
    half = x_packed.shape[1]
    lo, hi = _unpack_halves(x_packed)
    wg = w_gate.astype(BF16)
    wu = w_up.astype(BF16)
    g = (jnp.dot(lo, wg[:half, :], preferred_element_type=F32)
         + jnp.dot(hi, wg[half:, :], preferred_element_type=F32))
    u = (jnp.dot(lo, wu[:half, :], preferred_element_type=F32)
         + jnp.dot(hi, wu[half:, :], preferred_element_type=F32))
    a = (g * jax.nn.sigmoid(g) * u).astype(BF16)
    return jnp.dot(a, w_down.astype(BF16), preferred_element_type=F32)


def _expert_kernel(be_ref, used_ref, x_ref, wg_ref, wu_ref, wd_ref, o_ref):
    @pl.when(pl.program_id(0) < used_ref[0])
    def _():
        o_ref[...] = _pack_halves(_swiglu_packed(x_ref[...], wg_ref[...], wu_ref[...], wd_ref[...]))


def _experts(rows, block_expert, n_used, layer, w_gate, w_up, w_down):
    R, half = rows.shape
    D = 2 * half
    F = w_gate.shape[3]
    blk = EXPERT_BLOCK
    grid_spec = pltpu.PrefetchScalarGridSpec(
        num_scalar_prefetch=2,
        grid=(R // blk,),
        in_specs=[pl.BlockSpec((blk, half), lambda r, be, nu: (jnp.minimum(r, nu[0] - 1), 0)),
                  pl.BlockSpec((None, None, D, F), lambda r, be, nu: (layer, be[r], 0, 0)),
                  pl.BlockSpec((None, None, D, F), lambda r, be, nu: (layer, be[r], 0, 0)),
                  pl.BlockSpec((None, None, F, D), lambda r, be, nu: (layer, be[r], 0, 0))],
        out_specs=pl.BlockSpec((blk, half), lambda r, be, nu: (jnp.minimum(r, nu[0] - 1), 0)),
    )
    return pl.pallas_call(
        _expert_kernel,
        grid_spec=grid_spec,
        out_shape=jax.ShapeDtypeStruct((R, half), U32),
        compiler_params=_params("arbitrary"),
        name="routed_experts",
    )(block_expert, n_used, rows, w_gate, w_up, w_down)


def _moe_out_kernel(ga_ref, w_ref, h_ref, sg_ref, su_ref, sd_ref, x_ref, gate_ref, g_ref, b_ref, out_ref):
    half = h_ref.shape[1]
    y = _swiglu_packed(h_ref[...], sg_ref[...], su_ref[...], sd_ref[...])
    w = w_ref[...]
    y_lo = y[:, :half]
    y_hi = y[:, half:]
    for k in range(TOP_K):
        packed = ga_ref[k]
        wk = w[:, k:k + 1]
        y_lo = y_lo + wk * lax.bitcast_convert_type(packed << 16, F32)
        y_hi = y_hi + wk * lax.bitcast_convert_type(packed & jnp.uint32(0xFFFF0000), F32)
    y = jnp.concatenate([y_lo, y_hi], axis=1)
    z = DEEPNORM_ALPHA * x_ref[...] + gate_ref[...] * y
    out_ref[...] = _layer_norm(z, g_ref[...], b_ref[...])


def _moe_out_into_kernel(prev_ref, *refs):
    del prev_ref
    _moe_out_kernel(*refs)


def _moe_output_norm(gathered, wts, h, layer, s_gate, s_up, s_down, x, gate, ln_g, ln_b, out_batch, out_total, prev):
    B, S, D = x.shape
    half = D // 2
    F = s_gate.shape[2]
    tm = min(ROW_TILE // 2, S)
    row_spec = pl.BlockSpec((None, tm, D), lambda b, i: (b, i, 0))
    carried = () if prev is None else (prev,)
    return pl.pallas_call(
        _moe_out_kernel if prev is None else _moe_out_into_kernel,
        grid=(B, S // tm),
        in_specs=[pl.BlockSpec(memory_space=pl.ANY)] * len(carried)
        + [pl.BlockSpec((TOP_K, None, tm, half), lambda b, i: (0, b, i, 0)),
                  pl.BlockSpec((None, tm, TOP_K), lambda b, i: (b, i, 0)),
                  pl.BlockSpec((None, tm, half), lambda b, i: (b, i, 0)),
                  pl.BlockSpec((None, D, F), lambda b, i: (layer, 0, 0)),
                  pl.BlockSpec((None, D, F), lambda b, i: (layer, 0, 0)),
                  pl.BlockSpec((None, F, D), lambda b, i: (layer, 0, 0)),
                  row_spec,
                  pl.BlockSpec((None, 1, D), lambda b, i: (b, 0, 0)),
                  pl.BlockSpec((1, D), lambda b, i: (0, 0)),
                  pl.BlockSpec((1, D), lambda b, i: (0, 0))],
        out_specs=pl.BlockSpec((None, tm, D), lambda b, i: (b + out_batch, i, 0)),
        out_shape=jax.ShapeDtypeStruct((out_total, S, D), F32),
        input_output_aliases={} if prev is None else {0: 0},
        compiler_params=_params("parallel", "parallel"),
        name="moe_output_norm",
    )(*carried, gathered.reshape(TOP_K, B, S, half), wts, h, s_gate, s_up, s_down, x,
      gate.reshape(B, 1, D), ln_g.reshape(1, D), ln_b.reshape(1, D))


def _moe_layer(x, shift, scale, gate, router_w, router_bias, layer, w_gate, w_up, w_down,
               s_gate, s_up, s_down, ln_g, ln_b, out_batch=0, out_total=None, prev=None):
    B, S, D = x.shape
    T = B * S
    E = router_w.shape[1]
    blk = EXPERT_BLOCK
    h, idx, wts, pos, cnt = _router(x, shift, scale, router_w, router_bias)

    counts = cnt[:, 0].astype(I32)
    padded = (counts + blk - 1) // blk * blk
    pend = jnp.cumsum(padded)
    pstart = pend - padded
    n_blocks = -(-T * TOP_K // blk) + E
    onehot = idx[..., None] == jnp.arange(E, dtype=I32)
    dest = jnp.sum(jnp.where(onehot, pstart, 0), axis=-1) + pos
    block_expert = jnp.minimum(
        jnp.sum((jnp.arange(n_blocks, dtype=I32)[:, None] * blk >= pend[None, :]).astype(I32), axis=1),
        E - 1).astype(I32)
    n_used = (pend[-1:] // blk).astype(I32)

    rows = _sc_dispatch(h.reshape(T, D // 2), dest, n_blocks * blk)
    out_rows = _experts(rows, block_expert, n_used, layer, w_gate, w_up, w_down)
    gathered = _sc_gather(out_rows, dest.transpose(1, 0, 2).reshape(TOP_K * T))
    return _moe_output_norm(gathered, wts.transpose(0, 2, 1), h, layer, s_gate, s_up, s_down, x, gate, ln_g, ln_b,
                            out_batch, B if out_total is None else out_total, prev)


def kernel(x, c, fox_w_in, fox_b_f, fox_w_o, sb_w_in, sb_w_o, ada_w, ada_b, ln_attn_g, ln_attn_b,
           router_w, router_bias, exp_w_gate, exp_w_up, exp_w_down,
           shared_w_gate, shared_w_up, shared_w_down, ln_ffn_g, ln_ffn_b):
    B, S, D = x.shape
    H = D // HEAD_DIM
    depth = ada_w.shape[0]
    mod = _ada_mod(c, ada_w, ada_b)
    xs = [None] * B
    out = None
    for i in range(depth):
        j = i // 2
        w_in = fox_w_in[j] if i % 2 == 0 else sb_w_in[j]
        wq = (w_in[:, :D] * (ATTN_SCALE * LOG2_E if i % 2 == 0 else ATTN_SCALE)).astype(BF16)
        wk = w_in[:, D:2 * D].astype(BF16)
        wvt = w_in[:, 2 * D:3 * D].T.astype(BF16)
        if i % 2 == 0:
            wf = jnp.zeros((D, LANES), F32).at[:, :H].set(w_in[:, 3 * D:]).astype(BF16)
            bf = jnp.zeros((1, LANES), F32).at[0, :H].set(fox_b_f[j])
            w_o = fox_w_o[j].astype(BF16)
        else:
            w_o = sb_w_o[j].astype(BF16)
        for b in range(B):
            xb, x_batch = (x, b) if i == 0 else (xs[b], 0)
            sh_a, sc_a, g_a, sh_m, sc_m, g_m = jnp.split(mod[i, b:b + 1], 6, axis=-1)
            if i % 2 == 0:
                q, k_aug, vt, fcum, stats = _in_projection_forget(xb, x_batch, sh_a, sc_a, wq, wk, wvt, wf, bf)
                o = _fox_attention(q, k_aug, vt, fcum, stats, H)
            else:
                q, k, vt = _in_projection(xb, sh_a, sc_a, wq, wk, wvt)
                o = _sb_attention(q, k, vt, H)
            xb = _out_projection_norm(o, w_o, xb, x_batch, 1.0 + g_a, ln_attn_g[i], ln_attn_b[i])
            moe_args = (xb, sh_m, sc_m, 1.0 + g_m, router_w[i], router_bias[i], i, exp_w_gate, exp_w_up, exp_w_down,
                        shared_w_gate, shared_w_up, shared_w_down, ln_ffn_g[i], ln_ffn_b[i])
            if i + 1 < depth:
                xs[b] = _moe_layer(*moe_args)
            else:
                out = _moe_layer(*moe_args, out_batch=b, out_total=B, prev=out)
    return out
```
